```python
import math
import jax, jax.numpy as jnp
from jax import lax
import numpy as np

D_MODEL = 1024
BATCH = 8
SEQ = 2048
DEPTH = 4
DEC_BATCH = 128
DEC_SEQ = 8
PAST_LEN = 16384
PAGE_SIZE = 128

M_HEADS = 4
M_DK = D_MODEL // 8
M_DV = D_MODEL // M_HEADS
G_HEADS = 4
G_DK = D_MODEL // 8
G_DV = D_MODEL // G_HEADS
G_RANK = 16
G_TAU = 16.0
D_FF = 2688
CONV_W = 3
PLE_DIM = 256
CHUNK = 64
EPS = 1e-6

MQK_W = M_HEADS * M_DK
MV_W = M_HEADS * M_DV
GK_W = G_HEADS * G_DK
GV_W = G_HEADS * G_DV
IN_SIZES = [MQK_W, MQK_W, MV_W, M_HEADS, M_HEADS, MV_W,
            GK_W, GK_W, GV_W, G_RANK, GV_W,
            D_MODEL, D_MODEL]
N_IN = sum(IN_SIZES)
SPLIT_IDX = [int(v) for v in np.cumsum(IN_SIZES)[:-1]]

kernel_name = "mlstm_gla_gated_hybrid_step"


def rmsnorm(x, g):
    x32 = x.astype(jnp.float32)
    y = x32 * lax.rsqrt(jnp.mean(x32 * x32, axis=-1, keepdims=True) + EPS)
    return (y * g.astype(jnp.float32)).astype(x.dtype)


def head_layernorm(h):
    mu = jnp.mean(h, axis=-1, keepdims=True)
    var = jnp.mean(jnp.square(h - mu), axis=-1, keepdims=True)
    return (h - mu) * lax.rsqrt(var + EPS)


def head_rmsnorm(h):
    return h * lax.rsqrt(jnp.mean(h * h, axis=-1, keepdims=True) + EPS)


def chunk_len(S):
    return CHUNK if S % CHUNK == 0 else S


def to_chunks(t, L):
    B, S = t.shape[:2]
    return jnp.moveaxis(t.reshape((B, S // L, L) + t.shape[2:]), 1, 0)


def from_chunks(t):
    NC, B, L = t.shape[:3]
    return jnp.moveaxis(t, 0, 1).reshape((B, NC * L) + t.shape[3:])


def mlstm_scan(q, k, v, li, lf, C0, n0, m0):
    S = q.shape[1]
    L = chunk_len(S)
    tril = jnp.tril(jnp.ones((L, L), dtype=bool))

    def step(carry, inp):
        C, n, m = carry
        qc, kc, vc, lic, lfc = inp
        b = jnp.cumsum(lfc, axis=1)
        D = b[:, :, None, :] - b[:, None, :, :] + lic[:, None, :, :]
        D = jnp.where(tril[None, :, :, None], D, -jnp.inf)
        inter = b + m[:, None, :]
        mt = jnp.maximum(inter, jnp.max(D, axis=2))
        w_inter = jnp.exp(inter - mt)
        P = jnp.exp(D - mt[:, :, None, :])
        A = P * jnp.einsum('bthd,bshd->btsh', qc, kc)
        num = jnp.einsum('btsh,bshv->bthv', A, vc) + w_inter[..., None] * jnp.einsum('bthd,bhdv->bthv', qc, C)
        den = jnp.sum(A, axis=2) + w_inter * jnp.einsum('bthd,bhd->bth', qc, n)
        den = jnp.maximum(jnp.abs(den), jnp.exp(-mt))
        h = num / den[..., None]
        mL = mt[:, -1]
        ws = jnp.exp(b[:, -1][:, None, :] - b + lic - mL[:, None, :])
        carry_w = jnp.exp(b[:, -1] + m - mL)
        C_new = carry_w[..., None, None] * C + jnp.einsum('bsh,bshd,bshv->bhdv', ws, kc, vc)
        n_new = carry_w[..., None] * n + jnp.einsum('bsh,bshd->bhd', ws, kc)
        return (C_new, n_new, mL), h

    xs = (to_chunks(q, L), to_chunks(k, L), to_chunks(v, L), to_chunks(li, L), to_chunks(lf, L))
    (C, n, m), hs = lax.scan(step, (C0, n0, m0), xs)
    return from_chunks(hs), C, n, m


def gla_scan(q, k, v, la, S0):
    S = q.shape[1]
    L = chunk_len(S)
    tril = jnp.tril(jnp.ones((L, L), dtype=bool))

    def step(Sst, inp):
        qc, kc, vc, lac = inp
        Bc = jnp.cumsum(lac, axis=1)
        diff = Bc[:, :, None] - Bc[:, None, :]
        decay = jnp.exp(jnp.where(tril[None, :, :, None, None], diff, -jnp.inf))
        A = jnp.einsum('bthd,bshd,btshd->btsh', qc, kc, decay)
        o = jnp.einsum('btsh,bshv->bthv', A, vc) + jnp.einsum('bthd,bhdv->bthv', qc * jnp.exp(Bc), Sst)
        BL = Bc[:, -1]
        S_new = jnp.exp(BL)[..., None] * Sst + jnp.einsum('bshd,bshv->bhdv', kc * jnp.exp(BL[:, None] - Bc), vc)
        return S_new, o

    xs = (to_chunks(q, L), to_chunks(k, L), to_chunks(v, L), to_chunks(la, L))
    Sf, os_ = lax.scan(step, S0, xs)
    return from_chunks(os_), Sf


def token_mixers(a, W, i, C0, n0, m0, S0):
    B, S, _ = a.shape
    f32 = jnp.float32
    z = a @ W['w_in'][i]
    mq, mk, mv, mi, mf, mo, gq, gk, gv, glr, gg, ga, gb = jnp.split(z, SPLIT_IDX, axis=-1)
    mq = mq.reshape(B, S, M_HEADS, M_DK).astype(f32)
    mk = mk.reshape(B, S, M_HEADS, M_DK).astype(f32) * (M_DK ** -0.5)
    mv = mv.reshape(B, S, M_HEADS, M_DV).astype(f32)
    li = (mi + W['b_mi'][i]).astype(f32)
    lf = jax.nn.log_sigmoid((mf + W['b_mf'][i]).astype(f32))
    hm, C, n, m = mlstm_scan(mq, mk, mv, li, lf, C0.astype(f32), n0.astype(f32), m0.astype(f32))
    hm = head_layernorm(hm).reshape(B, S, MV_W) * W['g_mnorm'][i].astype(f32)
    hm = (hm * jax.nn.sigmoid(mo.astype(f32))).astype(a.dtype)
    gq = gq.reshape(B, S, G_HEADS, G_DK).astype(f32) * (G_DK ** -0.5)
    gk = gk.reshape(B, S, G_HEADS, G_DK).astype(f32)
    gv = gv.reshape(B, S, G_HEADS, G_DV).astype(f32)
    la = jax.nn.log_sigmoid((glr @ W['w_lr2'][i] + W['b_lr2'][i]).astype(f32)) / G_TAU
    la = la.reshape(B, S, G_HEADS, G_DK)
    hg, Sf = gla_scan(gq, gk, gv, la, S0.astype(f32))
    hg = head_rmsnorm(hg).reshape(B, S, GV_W) * W['g_gnorm'][i].astype(f32)
    hg = (hg * jax.nn.silu(gg.astype(f32))).astype(a.dtype)
    merged = jax.nn.sigmoid(ga) * (hm @ W['w_pa'][i]) + jax.nn.sigmoid(gb) * (hg @ W['w_pb'][i])
    return merged @ W['w_o'][i], C, n, m, Sf


def conv_ffn(c, W, i, buf):
    S = c.shape[1]
    uz = c @ W['w_up'][i]
    u_gate, u_val = jnp.split(uz, [D_FF], axis=-1)
    xp = jnp.concatenate([buf.astype(u_gate.dtype), u_gate], axis=1)
    cw = W['conv_w'][i]
    conv = W['conv_b'][i] + sum(cw[j] * xp[:, j:j + S] for j in range(CONV_W))
    act = jax.nn.silu(conv) * u_val
    return act @ W['w_down'][i], xp[:, -(CONV_W - 1):]


def trunk(x, p, C0, n0, m0, S0, cv0, W):
    h = x
    Cs, ns, ms, Ss, cvs = [], [], [], [], []
    for i in range(DEPTH):
        a = rmsnorm(h, W['norm_mix_g'][i])
        mix, C, n, m, Sf = token_mixers(a, W, i, C0[i], n0[i], m0[i], S0[i])
        h = h + mix
        c = rmsnorm(h, W['norm_ffn_g'][i])
        f, cv = conv_ffn(c, W, i, cv0[i])
        h = h + f
        e = rmsnorm(h, W['norm_ple_g'][i])
        h = h + jax.nn.sigmoid(e @ W['w_ple_gate'][i]) * (p[i] @ W['w_ple_proj'][i])
        Cs.append(C); ns.append(n); ms.append(m); Ss.append(Sf); cvs.append(cv)
    y = rmsnorm(h, W['final_norm_g'])
    return y, jnp.stack(Cs), jnp.stack(ns), jnp.stack(ms), jnp.stack(Ss), jnp.stack(cvs)


def setup_inputs(seed: int = 0) -> dict:
    key = jax.random.key(seed)
    ks = iter(jax.random.split(key, 40))

    def nrm(shape, scale):
        return jax.random.normal(next(ks), shape, jnp.float32) * scale

    def gain(shape):
        return 1.0 + nrm(shape, 0.05)

    d = {}
    d['x_prompt'] = nrm((BATCH, SEQ, D_MODEL), 1.0)
    d['x_sample'] = nrm((DEC_BATCH, DEC_SEQ, D_MODEL), 1.0)
    d['state_mlstm_C'] = nrm((DEPTH, DEC_BATCH, M_HEADS, M_DK, M_DV), 0.05)
    d['state_mlstm_n'] = nrm((DEPTH, DEC_BATCH, M_HEADS, M_DK), 0.5)
    d['state_mlstm_m'] = nrm((DEPTH, DEC_BATCH, M_HEADS), 1.0)
    d['state_gla_S'] = nrm((DEPTH, DEC_BATCH, G_HEADS, G_DK, G_DV), 0.1)
    d['state_ffn_conv'] = nrm((DEPTH, DEC_BATCH, CONV_W - 1, D_FF), 1.0)
    d['p_prompt'] = nrm((DEPTH, BATCH, SEQ, PLE_DIM), 1.0)
    d['p_sample'] = nrm((DEPTH, DEC_BATCH, DEC_SEQ, PLE_DIM), 1.0)
    d['norm_mix_g'] = gain((DEPTH, D_MODEL))
    d['w_in'] = nrm((DEPTH, D_MODEL, N_IN), D_MODEL ** -0.5)
    d['b_mi'] = nrm((DEPTH, M_HEADS), 0.1)
    d['b_mf'] = jnp.linspace(3.0, 6.0, M_HEADS, dtype=jnp.float32)[None, :] + nrm((DEPTH, M_HEADS), 0.1)
    d['g_mnorm'] = gain((DEPTH, MV_W))
    d['w_lr2'] = nrm((DEPTH, G_RANK, GK_W), G_RANK ** -0.5)
    d['b_lr2'] = nrm((DEPTH, GK_W), 0.1)
    d['g_gnorm'] = gain((DEPTH, GV_W))
    d['w_pa'] = nrm((DEPTH, MV_W, D_MODEL), MV_W ** -0.5)
    d['w_pb'] = nrm((DEPTH, GV_W, D_MODEL), GV_W ** -0.5)
    d['w_o'] = nrm((DEPTH, D_MODEL, D_MODEL), 0.5 * D_MODEL ** -0.5)
    d['norm_ffn_g'] = gain((DEPTH, D_MODEL))
    d['w_up'] = nrm((DEPTH, D_MODEL, 2 * D_FF), D_MODEL ** -0.5)
    d['conv_w'] = nrm((DEPTH, CONV_W, D_FF), CONV_W ** -0.5)
    d['conv_b'] = nrm((DEPTH, D_FF), 0.02)
    d['w_down'] = nrm((DEPTH, D_FF, D_MODEL), 0.5 * D_FF ** -0.5)
    d['norm_ple_g'] = gain((DEPTH, D_MODEL))
    d['w_ple_gate'] = nrm((DEPTH, D_MODEL, D_MODEL), D_MODEL ** -0.5)
    d['w_ple_proj'] = nrm((DEPTH, PLE_DIM, D_MODEL), 0.5 * PLE_DIM ** -0.5)
    d['final_norm_g'] = gain((D_MODEL,))
    return d


def reference(x_prompt, x_sample, state_mlstm_C, state_mlstm_n, state_mlstm_m, state_gla_S,
              state_ffn_conv, p_prompt, p_sample, norm_mix_g, w_in, b_mi, b_mf, g_mnorm, w_lr2,
              b_lr2, g_gnorm, w_pa, w_pb, w_o, norm_ffn_g, w_up, conv_w, conv_b, w_down,
              norm_ple_g, w_ple_gate, w_ple_proj, final_norm_g):
    W = dict(norm_mix_g=norm_mix_g, w_in=w_in, b_mi=b_mi, b_mf=b_mf, g_mnorm=g_mnorm,
             w_lr2=w_lr2, b_lr2=b_lr2, g_gnorm=g_gnorm, w_pa=w_pa, w_pb=w_pb, w_o=w_o,
             norm_ffn_g=norm_ffn_g, w_up=w_up, conv_w=conv_w, conv_b=conv_b, w_down=w_down,
             norm_ple_g=norm_ple_g, w_ple_gate=w_ple_gate, w_ple_proj=w_ple_proj,
             final_norm_g=final_norm_g)
    B = x_prompt.shape[0]
    f32 = jnp.float32
    C0 = jnp.zeros((DEPTH, B, M_HEADS, M_DK, M_DV), f32)
    n0 = jnp.zeros((DEPTH, B, M_HEADS, M_DK), f32)
    m0 = jnp.zeros((DEPTH, B, M_HEADS), f32)
    S0 = jnp.zeros((DEPTH, B, G_HEADS, G_DK, G_DV), f32)
    cv0 = jnp.zeros((DEPTH, B, CONV_W - 1, D_FF), x_prompt.dtype)
    y_prompt, C_p, n_p, m_p, S_p, conv_p = trunk(x_prompt, p_prompt, C0, n0, m0, S0, cv0, W)
    y_sample, C_s, n_s, m_s, S_s, conv_s = trunk(x_sample, p_sample, state_mlstm_C, state_mlstm_n,
                                                 state_mlstm_m, state_gla_S, state_ffn_conv, W)
    return (y_prompt, y_sample, C_p, n_p, m_p, S_p, conv_p, C_s, n_s, m_s, S_s, conv_s)
```

```python
import functools

import jax
import jax.numpy as jnp
from jax import lax
from jax.experimental import pallas as pl
from jax.experimental.pallas import tpu as pltpu

F32 = jnp.float32
BF16 = jnp.bfloat16
HIGHEST = lax.Precision.HIGHEST

D_MODEL = 1024
DEPTH = 4
HEADS = 4
DK = 128
DV = 256
G_RANK = 16
G_TAU = 16.0
D_FF = 2688
CONV_W = 3
PLE_DIM = 256
EPS = 1e-6
SCALE_K = DK ** -0.5

MQ, MK, MV, MO = 0, 512, 1024, 2048
GQ, GK, GV, GG = 3072, 3584, 4096, 5120
GA, GB = 6144, 7168
Z_BIG = 8192
Z_MIX = 6144
Z_SMALL = 128
LR_LANE = 8

LANE = 128
SUBLANE = 8
VMEM_LIMIT_BYTES = 56 * 1024 * 1024

ROW_TILE = 512
N_TILE = 1024
FF_TILE = 384
PROMPT_CHUNK = 64
SAMPLE_GROUP = 4


def _dot(a, b):
    return jnp.dot(a, b, preferred_element_type=F32)


def _dot_nt(a, b):
    return lax.dot_general(a, b, (((1,), (1,)), ((), ())), preferred_element_type=F32)


def _dot_tn(a, b):
    return lax.dot_general(a, b, (((0,), (0,)), ((), ())), preferred_element_type=F32)


def _log_sigmoid(x):
    return jnp.minimum(x, 0.0) - jnp.log1p(jnp.exp(-jnp.abs(x)))


def _sigmoid(x):
    return 1.0 / (1.0 + jnp.exp(-x))


def _rmsnorm(x, g):
    return x * lax.rsqrt(jnp.mean(x * x, axis=-1, keepdims=True) + EPS) * g


def _in_proj_kernel(h_ref, g_ref, wbig_ref, wsm_ref, zbig_ref, zsm_ref, a_scr):
    @pl.when(pl.program_id(1) == 0)
    def _():
        a = _rmsnorm(h_ref[...], g_ref[...]).astype(BF16)
        a_scr[...] = a
        zsm_ref[...] = _dot(a, wsm_ref[...])

    zbig_ref[...] = _dot(a_scr[...], wbig_ref[...]).astype(BF16)


def _in_proj(h, layer, W):
    T = h.shape[0]
    grid = (T // ROW_TILE, Z_BIG // N_TILE)
    return pl.pallas_call(
        _in_proj_kernel,
        grid=grid,
        in_specs=[
            pl.BlockSpec((ROW_TILE, D_MODEL), lambda i, j: (i, 0)),
            pl.BlockSpec((None, 1, D_MODEL), lambda i, j: (layer, 0, 0)),
            pl.BlockSpec((None, D_MODEL, N_TILE), lambda i, j: (layer, 0, j)),
            pl.BlockSpec((None, D_MODEL, Z_SMALL), lambda i, j: (layer, 0, 0)),
        ],
        out_specs=[
            pl.BlockSpec((ROW_TILE, N_TILE), lambda i, j: (i, j)),
            pl.BlockSpec((ROW_TILE, Z_SMALL), lambda i, j: (i, 0)),
        ],
        out_shape=[
            jax.ShapeDtypeStruct((T, Z_BIG), BF16),
            jax.ShapeDtypeStruct((T, Z_SMALL), F32),
        ],
        scratch_shapes=[pltpu.VMEM((ROW_TILE, D_MODEL), BF16)],
        compiler_params=pltpu.CompilerParams(
            dimension_semantics=("arbitrary", "arbitrary"),
            vmem_limit_bytes=VMEM_LIMIT_BYTES,
        ),
        name="in_proj",
    )(h, W["norm_mix_g"], W["w_big"], W["w_small"])


def _anchor(x, half):
    rows, width = x.shape
    if half >= SUBLANE:
        parts = []
        for blk in range(rows // (2 * half)):
            r = blk * 2 * half + half - 1
            parts.append(jnp.broadcast_to(x[r:r + 1, :], (2 * half, width)))
        return parts[0] if len(parts) == 1 else jnp.concatenate(parts, axis=0)
    x3 = x.reshape(rows // SUBLANE, SUBLANE, width)
    sub = lax.broadcasted_iota(jnp.int32, x3.shape, 1)

    def bc(i):
        return jnp.broadcast_to(x3[:, i:i + 1, :], x3.shape)

    if half == 4:
        out = bc(3)
    elif half == 2:
        out = jnp.where(sub < 4, bc(1), bc(5))
    else:
        out = jnp.where(sub < 2, bc(0), jnp.where(sub < 4, bc(2), jnp.where(sub < 6, bc(4), bc(6))))
    return out.reshape(rows, width)


def _mixer_kernel(zb_ref, zs_ref, c0_ref, n0_ref, m0_ref, s0_ref,
                  bias_ref, w2_ref, blr_ref, gm_ref, gg_ref,
                  hm_ref, hg_ref, c_ref, n_ref, m_ref, s_ref, *, L, seg, nc):
    nseg = L // seg
    seg_shift = seg.bit_length() - 1
    if nc > 1:
        @pl.when(pl.program_id(1) == 0)
        def _():
            c_ref[...] = c0_ref[...]
            n_ref[...] = n0_ref[...]
            m_ref[...] = m0_ref[...]
            s_ref[...] = s0_ref[...]
        cin, nin, min_, sin = c_ref, n_ref, m_ref, s_ref
    else:
        cin, nin, min_, sin = c0_ref, n0_ref, m0_ref, s0_ref

    row = lax.broadcasted_iota(jnp.int32, (L, L), 0)
    col = lax.broadcasted_iota(jnp.int32, (L, L), 1)
    causal = (col <= row) & ((row ^ col) < seg)
    tril_f = causal.astype(F32)
    eye = row == col
    lane = lax.broadcasted_iota(jnp.int32, (L, LANE), 1)
    rowl = lax.broadcasted_iota(jnp.int32, (L, LANE), 0)
    eye_k = (lax.broadcasted_iota(jnp.int32, (DK, DK), 0)
             == lax.broadcasted_iota(jnp.int32, (DK, DK), 1))
    lane4 = lax.broadcasted_iota(jnp.int32, (1, HEADS), 1)

    def seg_last(x):
        parts = [jnp.broadcast_to(x[(g + 1) * seg - 1:(g + 1) * seg, :], (seg, x.shape[1]))
                 for g in range(nseg)]
        return parts[0] if nseg == 1 else jnp.concatenate(parts, axis=0)

    def seg_rows(vals):
        parts = [jnp.broadcast_to(v, (seg, v.shape[1])) for v in vals]
        return parts[0] if nseg == 1 else jnp.concatenate(parts, axis=0)

    def seg_mask(x, g):
        return x if nseg == 1 else jnp.where((rowl >> seg_shift) == g, x, 0.0)

    def per_seg_dot(x_f32, state_ref, h):
        parts = [_dot(x_f32[g * seg:(g + 1) * seg, :].astype(BF16), state_ref[g, h].astype(BF16))
                 for g in range(nseg)]
        return parts[0] if nseg == 1 else jnp.concatenate(parts, axis=0)

    zs = zs_ref[...]
    gates = zs + bias_ref[...]
    lf = _log_sigmoid(gates)
    cum = jnp.dot(tril_f, lf, precision=HIGHEST, preferred_element_type=F32)
    cols = jnp.where(lane < HEADS, gates, cum)
    sel = (lax.broadcasted_iota(jnp.int32, (SUBLANE, LANE), 0)
           == lax.broadcasted_iota(jnp.int32, (SUBLANE, LANE), 1)).astype(F32)
    rows_ = lax.dot_general(sel, cols, (((1,), (1,)), ((), ())),
                            precision=HIGHEST, preferred_element_type=F32)

    m_old = [min_[g] for g in range(nseg)]
    m_new = [jnp.zeros((1, HEADS), F32) for _ in range(nseg)]

    for h in range(HEADS):
        q = zb_ref[:, MQ + DK * h:MQ + DK * (h + 1)]
        k = zb_ref[:, MK + DK * h:MK + DK * (h + 1)]
        v = zb_ref[:, MV + DV * h:MV + DV * (h + 1)]
        qf = q.astype(F32)
        li_c = cols[:, h:h + 1]
        b_c = cols[:, HEADS + h:HEADS + h + 1]
        li_r = rows_[h:h + 1, :]
        b_r = rows_[HEADS + h:HEADS + h + 1, :]
        m_col = seg_rows([m_old[g][:, h:h + 1] for g in range(nseg)])
        dmat = jnp.where(causal, b_c - b_r + li_r, -jnp.inf)
        inter = b_c + m_col
        mt = jnp.maximum(inter, jnp.max(dmat, axis=1, keepdims=True))
        w_inter = jnp.exp(inter - mt)
        a = jnp.exp(dmat - mt) * (_dot_nt(q, k) * SCALE_K)
        n_rows = seg_rows([nin[g, h:h + 1, :] for g in range(nseg)])
        qn = jnp.sum(qf * n_rows, axis=1, keepdims=True)
        num = _dot(a.astype(BF16), v) + w_inter * per_seg_dot(qf, cin, h)
        den = jnp.sum(a, axis=1, keepdims=True) + w_inter * qn
        den = jnp.maximum(jnp.abs(den), jnp.exp(-mt))
        hh = num * (1.0 / den)
        mu = jnp.mean(hh, axis=-1, keepdims=True)
        hc = hh - mu
        hn = hc * lax.rsqrt(jnp.mean(hc * hc, axis=-1, keepdims=True) + EPS)
        og = zb_ref[:, MO + DV * h:MO + DV * (h + 1)].astype(F32)
        hm_ref[:, DV * h:DV * (h + 1)] = (hn * gm_ref[:, DV * h:DV * (h + 1)] * _sigmoid(og)).astype(BF16)

        ws = jnp.exp(seg_last(b_c) - b_c + li_c - seg_last(mt)) * SCALE_K
        kw = k.astype(F32) * ws
        for g in range(nseg):
            last = (g + 1) * seg - 1
            m_last = mt[last:last + 1, :]
            cw = jnp.exp(b_c[last:last + 1, :] + m_old[g][:, h:h + 1] - m_last)
            c_new = cw * cin[g, h] + _dot_tn(seg_mask(kw, g).astype(BF16), v)
            n_new = cw * nin[g, h:h + 1, :] + jnp.sum(kw[g * seg:(g + 1) * seg, :], axis=0, keepdims=True)
            c_ref[g, h] = c_new
            n_ref[g, h:h + 1, :] = n_new
            m_new[g] = jnp.where(lane4 == h, m_last, m_new[g])
    for g in range(nseg):
        m_ref[g] = m_new[g]

    la = _log_sigmoid(_dot(zs.astype(BF16), w2_ref[...]) + blr_ref[...]) * (1.0 / G_TAU)
    bc = jnp.dot(tril_f, la, precision=HIGHEST, preferred_element_type=F32)
    qs = [zb_ref[:, GQ + DK * h:GQ + DK * (h + 1)].astype(F32) for h in range(HEADS)]
    ks = [zb_ref[:, GK + DK * h:GK + DK * (h + 1)].astype(F32) for h in range(HEADS)]
    amat = [jnp.where(eye, jnp.sum(qs[h] * ks[h], axis=1, keepdims=True), 0.0) for h in range(HEADS)]
    half = 1
    while half < seg:
        wgt = jnp.exp(-jnp.abs(bc - _anchor(bc, half)))
        second = (rowl & half) != 0
        for h in range(HEADS):
            w_h = wgt[:, DK * h:DK * (h + 1)]
            qt = jnp.where(second, qs[h] * w_h, 0.0).astype(BF16)
            kt = jnp.where(second, 0.0, ks[h] * w_h).astype(BF16)
            lvl = _dot_nt(qt, kt)
            if 2 * half < L:
                lvl = jnp.where((row ^ col) < 2 * half, lvl, 0.0)
            amat[h] = amat[h] + lvl
        half *= 2
    bl_rows = seg_last(bc)
    for h in range(HEADS):
        v = zb_ref[:, GV + DV * h:GV + DV * (h + 1)]
        b_h = bc[:, DK * h:DK * (h + 1)]
        qe = qs[h] * jnp.exp(b_h)
        o = _dot((amat[h] * SCALE_K).astype(BF16), v) + SCALE_K * per_seg_dot(qe, sin, h)
        on = o * lax.rsqrt(jnp.mean(o * o, axis=-1, keepdims=True) + EPS)
        gg = zb_ref[:, GG + DV * h:GG + DV * (h + 1)].astype(F32)
        hg_ref[:, DV * h:DV * (h + 1)] = (on * gg_ref[:, DV * h:DV * (h + 1)] * (gg * _sigmoid(gg))).astype(BF16)

        kt = ks[h] * jnp.exp(bl_rows[:, DK * h:DK * (h + 1)] - b_h)
        for g in range(nseg):
            last = (g + 1) * seg - 1
            dec_row = jnp.exp(b_h[last:last + 1, :])
            dec_col = jnp.sum(jnp.where(eye_k, jnp.broadcast_to(dec_row, (DK, DK)), 0.0),
                              axis=1, keepdims=True)
            s_ref[g, h] = dec_col * sin[g, h] + _dot_tn(seg_mask(kt, g).astype(BF16), v)


def _mixers(zbig, zsm, c0, n0, m0, s0, layer, W, *, L, seg, nc):
    T = zbig.shape[0]
    nseg = L // seg
    B = c0.shape[0]
    grid = (B // nseg, nc)
    row_map = lambda b, k: (b * nc + k, 0)
    st4 = lambda b, k: (b, 0, 0, 0)
    st3 = lambda b, k: (b, 0, 0)
    par = lambda b, k: (layer, 0, 0)
    state_specs = [
        pl.BlockSpec((nseg, HEADS, DK, DV), st4),
        pl.BlockSpec((nseg, HEADS, DK), st3),
        pl.BlockSpec((nseg, 1, HEADS), st3),
        pl.BlockSpec((nseg, HEADS, DK, DV), st4),
    ]
    return pl.pallas_call(
        functools.partial(_mixer_kernel, L=L, seg=seg, nc=nc),
        grid=grid,
        in_specs=[
            pl.BlockSpec((L, Z_MIX), row_map),
            pl.BlockSpec((L, Z_SMALL), row_map),
            *state_specs,
            pl.BlockSpec((None, 1, Z_SMALL), par),
            pl.BlockSpec((None, Z_SMALL, HEADS * DK), par),
            pl.BlockSpec((None, 1, HEADS * DK), par),
            pl.BlockSpec((None, 1, HEADS * DV), par),
            pl.BlockSpec((None, 1, HEADS * DV), par),
        ],
        out_specs=[
            pl.BlockSpec((L, HEADS * DV), row_map),
            pl.BlockSpec((L, HEADS * DV), row_map),
            *state_specs,
        ],
        out_shape=[
            jax.ShapeDtypeStruct((T, HEADS * DV), BF16),
            jax.ShapeDtypeStruct((T, HEADS * DV), BF16),
            jax.ShapeDtypeStruct(c0.shape, F32),
            jax.ShapeDtypeStruct(n0.shape, F32),
            jax.ShapeDtypeStruct(m0.shape, F32),
            jax.ShapeDtypeStruct(s0.shape, F32),
        ],
        compiler_params=pltpu.CompilerParams(
            dimension_semantics=("arbitrary", "arbitrary"),
            vmem_limit_bytes=VMEM_LIMIT_BYTES,
        ),
        name="mixers",
    )(zbig, zsm, c0, n0, m0, s0, W["gate_bias"], W["w_lr2"], W["b_lr2"], W["g_mnorm"], W["g_gnorm"])


def _channel_kernel(*refs, seq_tiles, has_state, final):
    (h_ref, hm_ref, hg_ref, ga_ref, gb_ref, wpa_ref, wpb_ref, wo_ref, gffn_ref,
     wug_ref, wuv_ref, cw_ref, cb_ref, wd_ref, gple_ref, wpg_ref, wpp_ref, p_ref) = refs[:18]
    pos = 18
    pst_ref = gfin_ref = None
    if has_state:
        pst_ref = refs[pos]
        pos += 1
    if final:
        gfin_ref = refs[pos]
        pos += 1
    out_ref, conv_ref = refs[pos:pos + 2]
    h1_scr, c_scr, acc_scr = refs[pos + 2:pos + 5]
    carry_scr = None if has_state else refs[pos + 5]

    i = pl.program_id(0)
    j = pl.program_id(1)
    tm = h_ref.shape[0]

    @pl.when(j == 0)
    def _():
        pa = _dot(hm_ref[...], wpa_ref[...])
        pb = _dot(hg_ref[...], wpb_ref[...])
        merged = _sigmoid(ga_ref[...].astype(F32)) * pa + _sigmoid(gb_ref[...].astype(F32)) * pb
        h1 = h_ref[...] + _dot(merged.astype(BF16), wo_ref[...])
        h1_scr[...] = h1
        c_scr[...] = _rmsnorm(h1, gffn_ref[...]).astype(BF16)
        acc_scr[...] = jnp.zeros_like(acc_scr)

    c = c_scr[...]
    ug = _dot(c, wug_ref[...])
    uv = _dot(c, wuv_ref[...])
    rowi = lax.broadcasted_iota(jnp.int32, ug.shape, 0)
    r1 = pltpu.roll(ug, 1, 0)
    r2 = pltpu.roll(ug, 2, 0)
    if has_state:
        pst = pst_ref[...]
        t = rowi & (SUBLANE - 1)
        u1 = jnp.where(t == 0, pltpu.roll(pst, tm - 1, 0), r1)
        u2 = jnp.where(t < 2, pst, r2)
        conv_ref[...] = ug
    else:
        prev = jnp.where(i % seq_tiles == 0, 0.0, carry_scr[j])
        p6 = jnp.broadcast_to(prev[6:7, :], ug.shape)
        p7 = jnp.broadcast_to(prev[7:8, :], ug.shape)
        u1 = jnp.where(rowi == 0, p7, r1)
        u2 = jnp.where(rowi == 0, p6, jnp.where(rowi == 1, p7, r2))
        tail = ug[tm - SUBLANE:, :]
        carry_scr[j] = tail
        conv_ref[...] = tail
    conv = cb_ref[...] + cw_ref[0:1, :] * u2 + cw_ref[1:2, :] * u1 + cw_ref[2:3, :] * ug
    act = conv * _sigmoid(conv) * uv
    acc_scr[...] += _dot(act.astype(BF16), wd_ref[...])

    @pl.when(j == pl.num_programs(1) - 1)
    def _():
        h2 = h1_scr[...] + acc_scr[...]
        e = _rmsnorm(h2, gple_ref[...]).astype(BF16)
        h3 = h2 + _sigmoid(_dot(e, wpg_ref[...])) * _dot(p_ref[...].astype(BF16), wpp_ref[...])
        if final:
            h3 = _rmsnorm(h3, gfin_ref[...])
        out_ref[...] = h3


def _channel(h, hm, hg, zbig, p, pst, layer, W, *, seq_rows, final):
    T = h.shape[0]
    tm = ROW_TILE
    nj = D_FF // FF_TILE
    has_state = pst is not None
    seq_tiles = max(seq_rows // tm, 1)
    rows = lambda i, j: (i, 0)
    par = lambda i, j: (layer, 0, 0)
    in_specs = [
        pl.BlockSpec((tm, D_MODEL), rows),
        pl.BlockSpec((tm, D_MODEL), rows),
        pl.BlockSpec((tm, D_MODEL), rows),
        pl.BlockSpec((tm, D_MODEL), lambda i, j: (i, GA // D_MODEL)),
        pl.BlockSpec((tm, D_MODEL), lambda i, j: (i, GB // D_MODEL)),
        pl.BlockSpec((None, D_MODEL, D_MODEL), par),
        pl.BlockSpec((None, D_MODEL, D_MODEL), par),
        pl.BlockSpec((None, D_MODEL, D_MODEL), par),
        pl.BlockSpec((None, 1, D_MODEL), par),
        pl.BlockSpec((None, D_MODEL, FF_TILE), lambda i, j: (layer, 0, j)),
        pl.BlockSpec((None, D_MODEL, FF_TILE), lambda i, j: (layer, 0, nj + j)),
        pl.BlockSpec((None, CONV_W, FF_TILE), lambda i, j: (layer, 0, j)),
        pl.BlockSpec((None, 1, FF_TILE), lambda i, j: (layer, 0, j)),
        pl.BlockSpec((None, FF_TILE, D_MODEL), lambda i, j: (layer, j, 0)),
        pl.BlockSpec((None, 1, D_MODEL), par),
        pl.BlockSpec((None, D_MODEL, D_MODEL), par),
        pl.BlockSpec((None, PLE_DIM, D_MODEL), par),
        pl.BlockSpec((None, tm, PLE_DIM), lambda i, j: (layer, i, 0)),
    ]
    args = [h, hm, hg, zbig, zbig, W["w_pa"], W["w_pb"], W["w_o"], W["norm_ffn_g"],
            W["w_up"], W["w_up"], W["conv_w"], W["conv_b"], W["w_down"], W["norm_ple_g"],
            W["w_ple_gate"], W["w_ple_proj"], p]
    scratch = [pltpu.VMEM((tm, D_MODEL), F32), pltpu.VMEM((tm, D_MODEL), BF16),
               pltpu.VMEM((tm, D_MODEL), F32)]
    if has_state:
        in_specs.append(pl.BlockSpec((tm, FF_TILE), lambda i, j: (i, j)))
        args.append(pst)
        conv_spec = pl.BlockSpec((tm, FF_TILE), lambda i, j: (i, j))
        conv_shape = jax.ShapeDtypeStruct((T, D_FF), F32)
    else:
        conv_spec = pl.BlockSpec((None, SUBLANE, FF_TILE), lambda i, j: (i, 0, j))
        conv_shape = jax.ShapeDtypeStruct((T // tm, SUBLANE, D_FF), F32)
        scratch.append(pltpu.VMEM((nj, SUBLANE, FF_TILE), F32))
    if final:
        in_specs.append(pl.BlockSpec((1, D_MODEL), lambda i, j: (0, 0)))
        args.append(W["final_norm_g"])
    return pl.pallas_call(
        functools.partial(_channel_kernel, seq_tiles=seq_tiles, has_state=has_state, final=final),
        grid=(T // tm, nj),
        in_specs=in_specs,
        out_specs=[pl.BlockSpec((tm, D_MODEL), rows), conv_spec],
        out_shape=[jax.ShapeDtypeStruct((T, D_MODEL), F32), conv_shape],
        scratch_shapes=scratch,
        compiler_params=pltpu.CompilerParams(
            dimension_semantics=("arbitrary", "arbitrary"),
            vmem_limit_bytes=VMEM_LIMIT_BYTES,
        ),
        name="channel",
    )(*args)


def _prep_weights(norm_mix_g, w_in, b_mi, b_mf, g_mnorm, w_lr2, b_lr2, g_gnorm, w_pa, w_pb, w_o,
                  norm_ffn_g, w_up, conv_w, conv_b, w_down, norm_ple_g, w_ple_gate, w_ple_proj,
                  final_norm_g):
    sizes = [512, 512, 1024, HEADS, HEADS, 1024, 512, 512, 1024, G_RANK, 1024, 1024, 1024]
    offs = [0]
    for s in sizes:
        offs.append(offs[-1] + s)
    mq, mk, mv, mi, mf, mo, gq, gk, gv, glr, gg, ga, gb = [
        w_in[:, :, offs[n]:offs[n + 1]] for n in range(len(sizes))]
    w_big = jnp.concatenate([mq, mk, mv, mo, gq, gk, gv, gg, ga, gb], axis=-1).astype(BF16)
    pad = Z_SMALL - 2 * HEADS - G_RANK
    w_small = jnp.concatenate([mi, mf, glr, jnp.zeros((DEPTH, D_MODEL, pad), F32)], axis=-1).astype(BF16)
    gate_bias = jnp.concatenate([b_mi, b_mf, jnp.zeros((DEPTH, Z_SMALL - 2 * HEADS), F32)], axis=-1)
    w_lr2_pad = jnp.zeros((DEPTH, Z_SMALL, HEADS * DK), F32).at[:, LR_LANE:LR_LANE + G_RANK, :].set(w_lr2)
    row = lambda x: x.reshape(DEPTH, 1, x.shape[-1])
    return dict(
        norm_mix_g=row(norm_mix_g), w_big=w_big, w_small=w_small,
        gate_bias=row(gate_bias), w_lr2=w_lr2_pad.astype(BF16), b_lr2=row(b_lr2),
        g_mnorm=row(g_mnorm), g_gnorm=row(g_gnorm),
        w_pa=w_pa.astype(BF16), w_pb=w_pb.astype(BF16), w_o=w_o.astype(BF16),
        norm_ffn_g=row(norm_ffn_g), w_up=w_up.astype(BF16), conv_w=conv_w, conv_b=row(conv_b),
        w_down=w_down.astype(BF16), norm_ple_g=row(norm_ple_g),
        w_ple_gate=w_ple_gate.astype(BF16), w_ple_proj=w_ple_proj.astype(BF16),
        final_norm_g=final_norm_g.reshape(1, D_MODEL),
    )


def _trunk(x, p, c0, n0, m0, s0, cv0, W, *, seq_rows):
    T = x.shape[0]
    B = T // seq_rows
    if seq_rows >= PROMPT_CHUNK:
        L, seg, nc = PROMPT_CHUNK, PROMPT_CHUNK, seq_rows // PROMPT_CHUNK
    else:
        L, seg, nc = SAMPLE_GROUP * seq_rows, seq_rows, 1
    h = x
    cs, ns, ms, ss, cvs = [], [], [], [], []
    for layer in range(DEPTH):
        zbig, zsm = _in_proj(h, layer, W)
        hm, hg, c, n, m, s = _mixers(zbig, zsm, c0[layer], n0[layer], m0[layer].reshape(B, 1, HEADS),
                                     s0[layer], layer, W, L=L, seg=seg, nc=nc)
        if cv0 is None:
            pst = None
        else:
            pst = jnp.pad(cv0[layer], ((0, 0), (0, seq_rows - (CONV_W - 1)), (0, 0))).reshape(T, D_FF)
        h, cv = _channel(h, hm, hg, zbig, p, pst, layer, W, seq_rows=seq_rows, final=layer == DEPTH - 1)
        if cv0 is None:
            cv = cv.reshape(B, -1, SUBLANE, D_FF)[:, -1, SUBLANE - (CONV_W - 1):, :]
        else:
            cv = cv.reshape(B, seq_rows, D_FF)[:, seq_rows - (CONV_W - 1):, :]
        cs.append(c)
        ns.append(n)
        ms.append(m.reshape(B, HEADS))
        ss.append(s)
        cvs.append(cv)
    return h, jnp.stack(cs), jnp.stack(ns), jnp.stack(ms), jnp.stack(ss), jnp.stack(cvs)


def kernel(x_prompt, x_sample, state_mlstm_C, state_mlstm_n, state_mlstm_m, state_gla_S, state_ffn_conv, p_prompt, p_sample, norm_mix_g, w_in, b_mi, b_mf, g_mnorm, w_lr2, b_lr2, g_gnorm, w_pa, w_pb, w_o, norm_ffn_g, w_up, conv_w, conv_b, w_down, norm_ple_g, w_ple_gate, w_ple_proj, final_norm_g):
    W = _prep_weights(norm_mix_g, w_in, b_mi, b_mf, g_mnorm, w_lr2, b_lr2, g_gnorm, w_pa, w_pb, w_o,
                      norm_ffn_g, w_up, conv_w, conv_b, w_down, norm_ple_g, w_ple_gate, w_ple_proj,
                      final_norm_g)
    bp, sp, _ = x_prompt.shape
    bs, ss, _ = x_sample.shape
    zc = jnp.zeros((bp, HEADS, DK, DV), F32)
    zn = jnp.zeros((bp, HEADS, DK), F32)
    zm = jnp.zeros((bp, HEADS), F32)
    yp, c_p, n_p, m_p, s_p, cv_p = _trunk(
        x_prompt.reshape(bp * sp, D_MODEL), p_prompt.reshape(DEPTH, bp * sp, PLE_DIM),
        [zc] * DEPTH, [zn] * DEPTH, [zm] * DEPTH, [zc] * DEPTH, None, W, seq_rows=sp)
    ys, c_s, n_s, m_s, s_s, cv_s = _trunk(
        x_sample.reshape(bs * ss, D_MODEL), p_sample.reshape(DEPTH, bs * ss, PLE_DIM),
        state_mlstm_C, state_mlstm_n, state_mlstm_m, state_gla_S, state_ffn_conv, W, seq_rows=ss)
    return (yp.reshape(bp, sp, D_MODEL), ys.reshape(bs, ss, D_MODEL),
            c_p, n_p, m_p, s_p, cv_p, c_s, n_s, m_s, s_s, cv_s)
```

```python
import functools

import jax
import jax.numpy as jnp
from jax import lax
from jax.experimental import pallas as pl
from jax.experimental.pallas import tpu as pltpu

F32 = jnp.float32
BF16 = jnp.bfloat16
HIGHEST = lax.Precision.HIGHEST

D_MODEL = 1024
DEPTH = 4
HEADS = 4
DK = 128
DV = 256
G_RANK = 16
G_TAU = 16.0
D_FF = 2688
CONV_W = 3
PLE_DIM = 256
EPS = 1e-6
SCALE_K = DK ** -0.5
N_IN = 8216

MQ, MK, MV, MO = 0, 512, 1024, 2048
GQ, GK, GV, GG = 3072, 3584, 4096, 5120
GA, GB = 6144, 7168
Z_BIG = 8192
Z_MIX = 6144
Z_SMALL = 128
LR_LANE = 8
SRC_MI, SRC_GLR = 2048, 5128
SHIFT_AFTER_MI = 2 * HEADS
SHIFT_AFTER_GLR = 2 * HEADS + G_RANK

LANE = 128
SUBLANE = 8
VMEM_LIMIT_BYTES = 58 * 1024 * 1024

PROJ_ROWS = 512
PROJ_COLS = 1024
CHAN_ROWS = 256
FF_CHUNK = 896
N_FF_CHUNKS = D_FF // FF_CHUNK
REPACK_COLS = 512
PROMPT_CHUNK = 128
SAMPLE_GROUP = 4
GLA_SAFE_LOG_DECAY = -60.0


def _dot(a, b):
    return jnp.dot(a, b, preferred_element_type=F32)


def _dot_nt(a, b):
    return lax.dot_general(a, b, (((1,), (1,)), ((), ())), preferred_element_type=F32)


def _dot_tn(a, b):
    return lax.dot_general(a, b, (((0,), (0,)), ((), ())), preferred_element_type=F32)


def _log_sigmoid(x):
    return jnp.minimum(x, 0.0) - jnp.log(1.0 + jnp.exp(-jnp.abs(x)))


def _masked_cumsum(mask_bf16, x, pieces):
    acc = None
    rest = x
    for n in range(pieces):
        piece = rest.astype(BF16)
        part = _dot(mask_bf16, piece)
        acc = part if acc is None else acc + part
        if n + 1 < pieces:
            rest = rest - piece.astype(F32)
    return acc


def _sigmoid(x):
    return 0.5 * jnp.tanh(0.5 * x) + 0.5


def _rmsnorm(x, g):
    return x * lax.rsqrt(jnp.mean(x * x, axis=-1, keepdims=True) + EPS) * g


def _resident(block_shape, index_map):
    return pl.BlockSpec(block_shape, index_map, pipeline_mode=pl.Buffered(1))


def _repack_kernel(a_ref, b_ref, o_ref):
    t = pl.program_id(1)
    width = REPACK_COLS + LANE
    first_after_mi = MO // REPACK_COLS
    first_after_glr = GG // REPACK_COLS

    def emit(shift):
        if shift == 0:
            y = a_ref[...]
        else:
            x = jnp.concatenate([a_ref[...], b_ref[...]], axis=1)
            y = pltpu.roll(x, width - shift, 1)[:, :REPACK_COLS]
        o_ref[...] = y.astype(BF16)

    @pl.when(t < first_after_mi)
    def _():
        emit(0)

    @pl.when((t >= first_after_mi) & (t < first_after_glr))
    def _():
        emit(SHIFT_AFTER_MI)

    @pl.when(t >= first_after_glr)
    def _():
        emit(SHIFT_AFTER_GLR)


def _repack_w_in(w_in):
    per_tile = REPACK_COLS // LANE
    return pl.pallas_call(
        _repack_kernel,
        grid=(DEPTH, Z_BIG // REPACK_COLS),
        in_specs=[
            pl.BlockSpec((None, D_MODEL, REPACK_COLS), lambda l, t: (l, 0, t)),
            pl.BlockSpec((None, D_MODEL, LANE), lambda l, t: (l, 0, per_tile * (t + 1))),
        ],
        out_specs=pl.BlockSpec((None, D_MODEL, REPACK_COLS), lambda l, t: (l, 0, t)),
        out_shape=jax.ShapeDtypeStruct((DEPTH, D_MODEL, Z_BIG), BF16),
        compiler_params=pltpu.CompilerParams(
            dimension_semantics=("arbitrary", "arbitrary"),
            vmem_limit_bytes=VMEM_LIMIT_BYTES,
        ),
        name="repack_w_in",
    )(w_in, w_in)


def _in_proj_kernel(h_ref, g_ref, wbig_ref, wsm_ref, zbig_ref, zsm_ref):
    a = _rmsnorm(h_ref[...], g_ref[...]).astype(BF16)
    zsm_ref[...] = _dot(a, wsm_ref[...])
    for n in range(Z_BIG // PROJ_COLS):
        cols = slice(n * PROJ_COLS, (n + 1) * PROJ_COLS)
        zbig_ref[:, cols] = _dot(a, wbig_ref[:, cols]).astype(BF16)


def _in_proj(h, layer, W):
    T = h.shape[0]
    par = lambda i: (layer, 0, 0)
    return pl.pallas_call(
        _in_proj_kernel,
        grid=(T // PROJ_ROWS,),
        in_specs=[
            pl.BlockSpec((PROJ_ROWS, D_MODEL), lambda i: (i, 0)),
            _resident((None, 1, D_MODEL), par),
            _resident((None, D_MODEL, Z_BIG), par),
            _resident((None, D_MODEL, Z_SMALL), par),
        ],
        out_specs=[
            pl.BlockSpec((PROJ_ROWS, Z_BIG), lambda i: (i, 0)),
            pl.BlockSpec((PROJ_ROWS, Z_SMALL), lambda i: (i, 0)),
        ],
        out_shape=[
            jax.ShapeDtypeStruct((T, Z_BIG), BF16),
            jax.ShapeDtypeStruct((T, Z_SMALL), F32),
        ],
        compiler_params=pltpu.CompilerParams(
            dimension_semantics=("arbitrary",),
            vmem_limit_bytes=VMEM_LIMIT_BYTES,
        ),
        name="in_proj",
    )(h, W["norm_mix_g"], W["w_big"], W["w_small"])


def _anchor(x, half):
    rows, width = x.shape
    if half >= SUBLANE:
        parts = []
        for blk in range(rows // (2 * half)):
            r = blk * 2 * half + half - 1
            parts.append(jnp.broadcast_to(x[r:r + 1, :], (2 * half, width)))
        return parts[0] if len(parts) == 1 else jnp.concatenate(parts, axis=0)
    x3 = x.reshape(rows // SUBLANE, SUBLANE, width)
    sub = lax.broadcasted_iota(jnp.int32, x3.shape, 1)

    def bc(i):
        return jnp.broadcast_to(x3[:, i:i + 1, :], x3.shape)

    if half == 4:
        out = bc(3)
    elif half == 2:
        out = jnp.where(sub < 4, bc(1), bc(5))
    else:
        out = jnp.where(sub < 2, bc(0), jnp.where(sub < 4, bc(2), jnp.where(sub < 6, bc(4), bc(6))))
    return out.reshape(rows, width)


def _mixer_kernel(*refs, L, seg, nc, aliased):
    (zb_ref, zs_ref, c0_ref, n0_ref, m0_ref, s0_ref,
     bias_ref, w2_ref, blr_ref, gm_ref, gg_ref) = refs[:11]
    outs = refs[11 + (4 if aliased else 0):]
    hm_ref, hg_ref, c_ref, n_ref, m_ref, s_ref = outs

    nseg = L // seg
    seg_shift = seg.bit_length() - 1
    if nc > 1:
        @pl.when(pl.program_id(1) == 0)
        def _():
            c_ref[...] = c0_ref[...]
            n_ref[...] = n0_ref[...]
            m_ref[...] = m0_ref[...]
            s_ref[...] = s0_ref[...]
        cin, nin, min_, sin = c_ref, n_ref, m_ref, s_ref
    else:
        cin, nin, min_, sin = c0_ref, n0_ref, m0_ref, s0_ref

    row = lax.broadcasted_iota(jnp.int32, (L, L), 0)
    col = lax.broadcasted_iota(jnp.int32, (L, L), 1)
    rxc = row ^ col
    causal = (col <= row) & (rxc < seg)
    tril_b = causal.astype(F32).astype(BF16)
    lane = lax.broadcasted_iota(jnp.int32, (L, LANE), 1)
    rowl = lax.broadcasted_iota(jnp.int32, (L, LANE), 0)
    eye_k = (lax.broadcasted_iota(jnp.int32, (DK, DK), 0)
             == lax.broadcasted_iota(jnp.int32, (DK, DK), 1))
    lane4 = lax.broadcasted_iota(jnp.int32, (1, HEADS), 1)

    def seg_last(x):
        parts = [jnp.broadcast_to(x[(g + 1) * seg - 1:(g + 1) * seg, :], (seg, x.shape[1]))
                 for g in range(nseg)]
        return parts[0] if nseg == 1 else jnp.concatenate(parts, axis=0)

    def seg_rows(vals):
        parts = [jnp.broadcast_to(v, (seg, v.shape[1])) for v in vals]
        return parts[0] if nseg == 1 else jnp.concatenate(parts, axis=0)

    def seg_mask(x, g):
        return x if nseg == 1 else jnp.where((rowl >> seg_shift) == g, x, 0.0)

    def per_seg_dot(x_f32, state_ref, h):
        parts = [_dot(x_f32[g * seg:(g + 1) * seg, :].astype(BF16), state_ref[g, h].astype(BF16))
                 for g in range(nseg)]
        return parts[0] if nseg == 1 else jnp.concatenate(parts, axis=0)

    zs = zs_ref[...]
    gates = zs + bias_ref[...]
    lf = _log_sigmoid(gates)
    cum = _masked_cumsum(tril_b, lf, 3)
    cols = jnp.where(lane < HEADS, gates, cum)
    sel = (lax.broadcasted_iota(jnp.int32, (SUBLANE, LANE), 0)
           == lax.broadcasted_iota(jnp.int32, (SUBLANE, LANE), 1)).astype(F32)
    rows_ = lax.dot_general(sel, cols, (((1,), (1,)), ((), ())),
                            precision=HIGHEST, preferred_element_type=F32)

    m_old = [min_[g] for g in range(nseg)]
    m_new = [jnp.zeros((1, HEADS), F32) for _ in range(nseg)]

    for h in range(HEADS):
        q = zb_ref[:, MQ + DK * h:MQ + DK * (h + 1)]
        k = zb_ref[:, MK + DK * h:MK + DK * (h + 1)]
        v = zb_ref[:, MV + DV * h:MV + DV * (h + 1)]
        qf = q.astype(F32)
        li_c = cols[:, h:h + 1]
        b_c = cols[:, HEADS + h:HEADS + h + 1]
        li_r = rows_[h:h + 1, :]
        b_r = rows_[HEADS + h:HEADS + h + 1, :]
        m_col = seg_rows([m_old[g][:, h:h + 1] for g in range(nseg)])
        dmat = jnp.where(causal, b_c + (li_r - b_r), -jnp.inf)
        inter = b_c + m_col
        mt = jnp.maximum(inter, jnp.max(dmat, axis=1, keepdims=True))
        w_inter = jnp.exp(inter - mt)
        a = jnp.exp(dmat - mt) * (_dot_nt(q, k) * SCALE_K)
        n_rows = seg_rows([nin[g, h:h + 1, :] for g in range(nseg)])
        qn = jnp.sum(qf * n_rows, axis=1, keepdims=True)
        num = _dot(a.astype(BF16), v) + w_inter * per_seg_dot(qf, cin, h)
        den = jnp.sum(a, axis=1, keepdims=True) + w_inter * qn
        den = jnp.maximum(jnp.abs(den), jnp.exp(-mt))
        hh = num * (1.0 / den)
        mu = jnp.mean(hh, axis=-1, keepdims=True)
        hc = hh - mu
        hn = hc * lax.rsqrt(jnp.mean(hc * hc, axis=-1, keepdims=True) + EPS)
        og = zb_ref[:, MO + DV * h:MO + DV * (h + 1)].astype(F32)
        hm_ref[:, DV * h:DV * (h + 1)] = (hn * gm_ref[:, DV * h:DV * (h + 1)] * _sigmoid(og)).astype(BF16)

        ws = jnp.exp(seg_last(b_c) - b_c + li_c - seg_last(mt)) * SCALE_K
        kw = k.astype(F32) * ws
        for g in range(nseg):
            last = (g + 1) * seg - 1
            m_last = mt[last:last + 1, :]
            cw = jnp.exp(b_c[last:last + 1, :] + m_old[g][:, h:h + 1] - m_last)
            c_new = cw * cin[g, h] + _dot_tn(seg_mask(kw, g).astype(BF16), v)
            n_new = cw * nin[g, h:h + 1, :] + jnp.sum(kw[g * seg:(g + 1) * seg, :], axis=0, keepdims=True)
            c_ref[g, h] = c_new
            n_ref[g, h:h + 1, :] = n_new
            m_new[g] = jnp.where(lane4 == h, m_last, m_new[g])
    for g in range(nseg):
        m_ref[g] = m_new[g]

    la = _log_sigmoid(_dot(zs.astype(BF16), w2_ref[...]) + blr_ref[...]) * (1.0 / G_TAU)
    bc = _masked_cumsum(tril_b, la, 2)
    qs = [zb_ref[:, GQ + DK * h:GQ + DK * (h + 1)].astype(F32) for h in range(HEADS)]
    ks = [zb_ref[:, GK + DK * h:GK + DK * (h + 1)].astype(F32) for h in range(HEADS)]
    qe = [qs[h] * jnp.exp(bc[:, DK * h:DK * (h + 1)]) for h in range(HEADS)]

    def intra_single_anchor():
        out = []
        for h in range(HEADS):
            ke = ks[h] * jnp.exp(-bc[:, DK * h:DK * (h + 1)])
            out.append(jnp.where(causal, _dot_nt(qe[h].astype(BF16), ke.astype(BF16)), 0.0))
        return out

    def intra_by_levels():
        out = [jnp.where(row == col, jnp.sum(qs[h] * ks[h], axis=1, keepdims=True), 0.0)
               for h in range(HEADS)]
        half = 1
        while half < seg:
            wgt = jnp.exp(-jnp.abs(bc - _anchor(bc, half)))
            pair = (col < row) & (rxc >= half) & (rxc < 2 * half)
            for h in range(HEADS):
                w_h = wgt[:, DK * h:DK * (h + 1)]
                lvl = _dot_nt((qs[h] * w_h).astype(BF16), (ks[h] * w_h).astype(BF16))
                out[h] = jnp.where(pair, lvl, out[h])
            half *= 2
        return out

    amat = lax.cond(jnp.min(bc) >= GLA_SAFE_LOG_DECAY, intra_single_anchor, intra_by_levels)

    bl_rows = seg_last(bc)
    for h in range(HEADS):
        v = zb_ref[:, GV + DV * h:GV + DV * (h + 1)]
        b_h = bc[:, DK * h:DK * (h + 1)]
        o = _dot((amat[h] * SCALE_K).astype(BF16), v) + SCALE_K * per_seg_dot(qe[h], sin, h)
        on = o * lax.rsqrt(jnp.mean(o * o, axis=-1, keepdims=True) + EPS)
        gg = zb_ref[:, GG + DV * h:GG + DV * (h + 1)].astype(F32)
        hg_ref[:, DV * h:DV * (h + 1)] = (on * gg_ref[:, DV * h:DV * (h + 1)] * (gg * _sigmoid(gg))).astype(BF16)

        kt = ks[h] * jnp.exp(bl_rows[:, DK * h:DK * (h + 1)] - b_h)
        for g in range(nseg):
            last = (g + 1) * seg - 1
            dec_row = jnp.exp(b_h[last:last + 1, :])
            dec_col = jnp.sum(jnp.where(eye_k, jnp.broadcast_to(dec_row, (DK, DK)), 0.0),
                              axis=1, keepdims=True)
            s_ref[g, h] = dec_col * sin[g, h] + _dot_tn(seg_mask(kt, g).astype(BF16), v)


def _mixers(zbig, zsm, states_in, in_layer, states_out, layer, W, *, L, seg, nc):
    T = zbig.shape[0]
    nseg = L // seg
    B = states_in[0].shape[1]
    grid = (B // nseg, nc)
    row_map = lambda b, k: (b * nc + k, 0)
    par = lambda b, k: (layer, 0, 0)

    def state_specs(at):
        return [
            pl.BlockSpec((None, nseg, HEADS, DK, DV), lambda b, k: (at, b, 0, 0, 0)),
            pl.BlockSpec((None, nseg, HEADS, DK), lambda b, k: (at, b, 0, 0)),
            pl.BlockSpec((None, nseg, 1, HEADS), lambda b, k: (at, b, 0, 0)),
            pl.BlockSpec((None, nseg, HEADS, DK, DV), lambda b, k: (at, b, 0, 0, 0)),
        ]

    in_specs = [
        pl.BlockSpec((L, Z_MIX), row_map),
        pl.BlockSpec((L, Z_SMALL), row_map),
        *state_specs(in_layer),
        pl.BlockSpec((None, 1, Z_SMALL), par),
        pl.BlockSpec((None, Z_SMALL, HEADS * DK), par),
        pl.BlockSpec((None, 1, HEADS * DK), par),
        pl.BlockSpec((None, 1, HEADS * DV), par),
        pl.BlockSpec((None, 1, HEADS * DV), par),
    ]
    args = [zbig, zsm, *states_in, W["gate_bias"], W["w_lr2"], W["b_lr2"], W["g_mnorm"], W["g_gnorm"]]
    aliases = {}
    if states_out is not None:
        for n, s in enumerate(states_out):
            aliases[len(args)] = 2 + n
            in_specs.append(pl.BlockSpec(memory_space=pl.ANY))
            args.append(s)
    out_shape = [
        jax.ShapeDtypeStruct((T, HEADS * DV), BF16),
        jax.ShapeDtypeStruct((T, HEADS * DV), BF16),
        jax.ShapeDtypeStruct((DEPTH, B, HEADS, DK, DV), F32),
        jax.ShapeDtypeStruct((DEPTH, B, HEADS, DK), F32),
        jax.ShapeDtypeStruct((DEPTH, B, 1, HEADS), F32),
        jax.ShapeDtypeStruct((DEPTH, B, HEADS, DK, DV), F32),
    ]
    outs = pl.pallas_call(
        functools.partial(_mixer_kernel, L=L, seg=seg, nc=nc, aliased=states_out is not None),
        grid=grid,
        in_specs=in_specs,
        out_specs=[
            pl.BlockSpec((L, HEADS * DV), row_map),
            pl.BlockSpec((L, HEADS * DV), row_map),
            *state_specs(layer),
        ],
        out_shape=out_shape,
        input_output_aliases=aliases,
        compiler_params=pltpu.CompilerParams(
            dimension_semantics=("arbitrary", "arbitrary"),
            vmem_limit_bytes=VMEM_LIMIT_BYTES,
        ),
        name="mixers",
    )(*args)
    return outs[0], outs[1], tuple(outs[2:])


def _channel_kernel(*refs, seq_tiles, has_state, final):
    (h_ref, hm_ref, hg_ref, ga_ref, gb_ref, wpa_ref, wpb_ref, wo_ref, gffn_ref,
     wup_ref, cw_ref, cb_ref, wd_ref, gple_ref, wpg_ref, wpp_ref, p_ref) = refs[:17]
    pos = 17
    pst_ref = gfin_ref = carry_scr = None
    if has_state:
        pst_ref = refs[pos]
        pos += 1
    if final:
        gfin_ref = refs[pos]
        pos += 1
    out_ref, conv_ref = refs[pos:pos + 2]
    if not has_state:
        carry_scr = refs[pos + 2]

    i = pl.program_id(0)
    tm = h_ref.shape[0]

    pa = _dot(hm_ref[...], wpa_ref[...])
    pb = _dot(hg_ref[...], wpb_ref[...])
    merged = _sigmoid(ga_ref[...].astype(F32)) * pa + _sigmoid(gb_ref[...].astype(F32)) * pb
    h1 = h_ref[...] + _dot(merged.astype(BF16), wo_ref[...])
    c = _rmsnorm(h1, gffn_ref[...]).astype(BF16)

    rowi = lax.broadcasted_iota(jnp.int32, (tm, FF_CHUNK), 0)
    acc = None
    for n in range(N_FF_CHUNKS):
        ff = slice(n * FF_CHUNK, (n + 1) * FF_CHUNK)
        uz = _dot(c, wup_ref[:, 2 * n * FF_CHUNK:2 * (n + 1) * FF_CHUNK])
        ug = uz[:, :FF_CHUNK]
        uv = uz[:, FF_CHUNK:]
        r1 = pltpu.roll(ug, 1, 0)
        r2 = pltpu.roll(ug, 2, 0)
        if has_state:
            pst = pst_ref[:, ff]
            t = rowi & (SUBLANE - 1)
            u1 = jnp.where(t == 0, pltpu.roll(pst, tm - 1, 0), r1)
            u2 = jnp.where(t < 2, pst, r2)
            conv_ref[:, ff] = ug
        else:
            prev = jnp.where(i % seq_tiles == 0, 0.0, carry_scr[:, ff])
            p6 = jnp.broadcast_to(prev[6:7, :], ug.shape)
            p7 = jnp.broadcast_to(prev[7:8, :], ug.shape)
            u1 = jnp.where(rowi == 0, p7, r1)
            u2 = jnp.where(rowi == 0, p6, jnp.where(rowi == 1, p7, r2))
            tail = ug[tm - SUBLANE:, :]
            carry_scr[:, ff] = tail
            conv_ref[:, ff] = tail
        conv = cb_ref[:, ff] + cw_ref[0:1, ff] * u2 + cw_ref[1:2, ff] * u1 + cw_ref[2:3, ff] * ug
        act = conv * _sigmoid(conv) * uv
        part = _dot(act.astype(BF16), wd_ref[ff, :])
        acc = part if acc is None else acc + part

    h2 = h1 + acc
    e = _rmsnorm(h2, gple_ref[...]).astype(BF16)
    h3 = h2 + _sigmoid(_dot(e, wpg_ref[...])) * _dot(p_ref[...].astype(BF16), wpp_ref[...])
    if final:
        h3 = _rmsnorm(h3, gfin_ref[...])
    out_ref[...] = h3


def _channel(h, hm, hg, zbig, p, pst, layer, W, *, seq_rows, final):
    T = h.shape[0]
    tm = CHAN_ROWS
    has_state = pst is not None
    seq_tiles = max(seq_rows // tm, 1)
    rows = lambda i: (i, 0)
    par = lambda i: (layer, 0, 0)
    in_specs = [
        pl.BlockSpec((tm, D_MODEL), rows),
        pl.BlockSpec((tm, D_MODEL), rows),
        pl.BlockSpec((tm, D_MODEL), rows),
        pl.BlockSpec((tm, D_MODEL), lambda i: (i, GA // D_MODEL)),
        pl.BlockSpec((tm, D_MODEL), lambda i: (i, GB // D_MODEL)),
        _resident((None, D_MODEL, D_MODEL), par),
        _resident((None, D_MODEL, D_MODEL), par),
        _resident((None, D_MODEL, D_MODEL), par),
        _resident((None, 1, D_MODEL), par),
        _resident((None, D_MODEL, 2 * D_FF), par),
        _resident((None, CONV_W, D_FF), par),
        _resident((None, 1, D_FF), par),
        _resident((None, D_FF, D_MODEL), par),
        _resident((None, 1, D_MODEL), par),
        _resident((None, D_MODEL, D_MODEL), par),
        _resident((None, PLE_DIM, D_MODEL), par),
        pl.BlockSpec((None, tm, PLE_DIM), lambda i: (layer, i, 0)),
    ]
    args = [h, hm, hg, zbig, zbig, W["w_pa"], W["w_pb"], W["w_o"], W["norm_ffn_g"],
            W["w_up"], W["conv_w"], W["conv_b"], W["w_down"], W["norm_ple_g"],
            W["w_ple_gate"], W["w_ple_proj"], p]
    scratch = []
    if has_state:
        in_specs.append(pl.BlockSpec((tm, D_FF), rows))
        args.append(pst)
        conv_spec = pl.BlockSpec((tm, D_FF), rows)
        conv_shape = jax.ShapeDtypeStruct((T, D_FF), F32)
    else:
        conv_spec = pl.BlockSpec((None, SUBLANE, D_FF), lambda i: (i, 0, 0))
        conv_shape = jax.ShapeDtypeStruct((T // tm, SUBLANE, D_FF), F32)
        scratch.append(pltpu.VMEM((SUBLANE, D_FF), F32))
    if final:
        in_specs.append(_resident((1, D_MODEL), lambda i: (0, 0)))
        args.append(W["final_norm_g"])
    return pl.pallas_call(
        functools.partial(_channel_kernel, seq_tiles=seq_tiles, has_state=has_state, final=final),
        grid=(T // tm,),
        in_specs=in_specs,
        out_specs=[pl.BlockSpec((tm, D_MODEL), rows), conv_spec],
        out_shape=[jax.ShapeDtypeStruct((T, D_MODEL), F32), conv_shape],
        scratch_shapes=scratch,
        compiler_params=pltpu.CompilerParams(
            dimension_semantics=("arbitrary",),
            vmem_limit_bytes=VMEM_LIMIT_BYTES,
        ),
        name="channel",
    )(*args)


def _prep_weights(norm_mix_g, w_in, b_mi, b_mf, g_mnorm, w_lr2, b_lr2, g_gnorm, w_pa, w_pb, w_o,
                  norm_ffn_g, w_up, conv_w, conv_b, w_down, norm_ple_g, w_ple_gate, w_ple_proj,
                  final_norm_g):
    narrow = jnp.concatenate([w_in[:, :, SRC_MI:SRC_MI + 2 * HEADS], w_in[:, :, SRC_GLR:SRC_GLR + G_RANK]], axis=-1)
    w_small = jnp.pad(narrow, ((0, 0), (0, 0), (0, Z_SMALL - narrow.shape[-1]))).astype(BF16)
    gate_bias = jnp.concatenate([b_mi, b_mf, jnp.zeros((DEPTH, Z_SMALL - 2 * HEADS), F32)], axis=-1)
    w_lr2_pad = jnp.zeros((DEPTH, Z_SMALL, HEADS * DK), F32).at[:, LR_LANE:LR_LANE + G_RANK, :].set(w_lr2)
    w_up_pairs = w_up.astype(BF16).reshape(DEPTH, D_MODEL, 2, N_FF_CHUNKS, FF_CHUNK)
    w_up_pairs = jnp.swapaxes(w_up_pairs, 2, 3).reshape(DEPTH, D_MODEL, 2 * D_FF)
    row = lambda x: x.reshape(DEPTH, 1, x.shape[-1])
    return dict(
        norm_mix_g=row(norm_mix_g), w_big=_repack_w_in(w_in), w_small=w_small,
        gate_bias=row(gate_bias), w_lr2=w_lr2_pad.astype(BF16), b_lr2=row(b_lr2),
        g_mnorm=row(g_mnorm), g_gnorm=row(g_gnorm),
        w_pa=w_pa.astype(BF16), w_pb=w_pb.astype(BF16), w_o=w_o.astype(BF16),
        norm_ffn_g=row(norm_ffn_g), w_up=w_up_pairs, conv_w=conv_w, conv_b=row(conv_b),
        w_down=w_down.astype(BF16), norm_ple_g=row(norm_ple_g),
        w_ple_gate=w_ple_gate.astype(BF16), w_ple_proj=w_ple_proj.astype(BF16),
        final_norm_g=final_norm_g.reshape(1, D_MODEL),
    )


def _trunk(x, p, states, stacked, cv0, W, *, seq_rows):
    T = x.shape[0]
    B = T // seq_rows
    if seq_rows >= PROMPT_CHUNK:
        L, seg, nc = PROMPT_CHUNK, PROMPT_CHUNK, seq_rows // PROMPT_CHUNK
    else:
        L, seg, nc = SAMPLE_GROUP * seq_rows, seq_rows, 1
    h = x
    new_states = None
    cvs = []
    for layer in range(DEPTH):
        zbig, zsm = _in_proj(h, layer, W)
        hm, hg, new_states = _mixers(zbig, zsm, states, layer if stacked else 0, new_states, layer, W,
                                     L=L, seg=seg, nc=nc)
        if cv0 is None:
            pst = None
        else:
            pst = jnp.pad(cv0[layer], ((0, 0), (0, seq_rows - (CONV_W - 1)), (0, 0))).reshape(T, D_FF)
        h, cv = _channel(h, hm, hg, zbig, p, pst, layer, W, seq_rows=seq_rows, final=layer == DEPTH - 1)
        if cv0 is None:
            cv = cv.reshape(B, -1, SUBLANE, D_FF)[:, -1, SUBLANE - (CONV_W - 1):, :]
        else:
            cv = cv.reshape(B, seq_rows, D_FF)[:, seq_rows - (CONV_W - 1):, :]
        cvs.append(cv)
    c, n, m, s = new_states
    return h, c, n, m.reshape(DEPTH, B, HEADS), s, jnp.stack(cvs)


def kernel(x_prompt, x_sample, state_mlstm_C, state_mlstm_n, state_mlstm_m, state_gla_S, state_ffn_conv, p_prompt, p_sample, norm_mix_g, w_in, b_mi, b_mf, g_mnorm, w_lr2, b_lr2, g_gnorm, w_pa, w_pb, w_o, norm_ffn_g, w_up, conv_w, conv_b, w_down, norm_ple_g, w_ple_gate, w_ple_proj, final_norm_g):
    W = _prep_weights(norm_mix_g, w_in, b_mi, b_mf, g_mnorm, w_lr2, b_lr2, g_gnorm, w_pa, w_pb, w_o,
                      norm_ffn_g, w_up, conv_w, conv_b, w_down, norm_ple_g, w_ple_gate, w_ple_proj,
                      final_norm_g)
    bp, sp, _ = x_prompt.shape
    bs, ss, _ = x_sample.shape
    zero_states = (jnp.zeros((1, bp, HEADS, DK, DV), F32), jnp.zeros((1, bp, HEADS, DK), F32),
                   jnp.zeros((1, bp, 1, HEADS), F32), jnp.zeros((1, bp, HEADS, DK, DV), F32))
    yp, c_p, n_p, m_p, s_p, cv_p = _trunk(
        x_prompt.reshape(bp * sp, D_MODEL), p_prompt.reshape(DEPTH, bp * sp, PLE_DIM),
        zero_states, False, None, W, seq_rows=sp)
    sample_states = (state_mlstm_C, state_mlstm_n, state_mlstm_m.reshape(DEPTH, bs, 1, HEADS), state_gla_S)
    ys, c_s, n_s, m_s, s_s, cv_s = _trunk(
        x_sample.reshape(bs * ss, D_MODEL), p_sample.reshape(DEPTH, bs * ss, PLE_DIM),
        sample_states, True, state_ffn_conv, W, seq_rows=ss)
    return (yp.reshape(bp, sp, D_MODEL), ys.reshape(bs, ss, D_MODEL),
            c_p, n_p, m_p, s_p, cv_p, c_s, n_s, m_s, s_s, cv_s)
```

```python
import functools

import jax
import jax.numpy as jnp
from jax import lax
from jax.experimental import pallas as pl
from jax.experimental.pallas import tpu as pltpu

F32 = jnp.float32
BF16 = jnp.bfloat16
HIGHEST = lax.Precision.HIGHEST

D_MODEL = 1024
DEPTH = 4
HEADS = 4
DK = 128
DV = 256
G_RANK = 16
G_TAU = 16.0
D_FF = 2688
CONV_W = 3
PLE_DIM = 256
EPS = 1e-6
SCALE_K = DK ** -0.5

MQ, MK, MV, MO = 0, 512, 1024, 2048
GQ, GK, GV, GG = 3072, 3584, 4096, 5120
GA, GB = 6144, 7168
Z_BIG = 8192
Z_MIX = 6144
Z_GATES = Z_BIG - Z_MIX
Z_SMALL = 128
LR_LANE = 8
SRC_MI, SRC_GLR = 2048, 5128
SHIFT_AFTER_MI = 2 * HEADS
SHIFT_AFTER_GLR = 2 * HEADS + G_RANK

LANE = 128
SUBLANE = 8
VMEM_LIMIT_BYTES = 58 * 1024 * 1024

PROJ_COLS = 1024
CHAN_ROWS = 256
UP_CHUNK = 1792
FF_CHUNK = 896
REPACK_COLS = 512
PROMPT_CHUNK = 128
SAMPLE_GROUP = 4
GLA_SAFE_LOG_DECAY = -60.0


def _dot(a, b):
    return jnp.dot(a, b, preferred_element_type=F32)


def _dot_nt(a, b):
    return lax.dot_general(a, b, (((1,), (1,)), ((), ())), preferred_element_type=F32)


def _dot_tn(a, b):
    return lax.dot_general(a, b, (((0,), (0,)), ((), ())), preferred_element_type=F32)


def _log_sigmoid(x):
    return jnp.minimum(x, 0.0) - jnp.log(1.0 + jnp.exp(-jnp.abs(x)))


def _masked_cumsum(mask_bf16, x, pieces):
    acc = None
    rest = x
    for n in range(pieces):
        piece = rest.astype(BF16)
        part = _dot(mask_bf16, piece)
        acc = part if acc is None else acc + part
        if n + 1 < pieces:
            rest = rest - piece.astype(F32)
    return acc


def _sigmoid(x):
    return 0.5 * jnp.tanh(0.5 * x) + 0.5


def _rmsnorm(x, g):
    return x * lax.rsqrt(jnp.mean(x * x, axis=-1, keepdims=True) + EPS) * g


def _resident(block_shape, index_map):
    return pl.BlockSpec(block_shape, index_map, pipeline_mode=pl.Buffered(1))


def _repack_kernel(a_ref, b_ref, o_ref):
    t = pl.program_id(1)
    width = REPACK_COLS + LANE
    first_after_mi = MO // REPACK_COLS
    first_after_glr = GG // REPACK_COLS

    def emit(shift):
        if shift == 0:
            y = a_ref[...]
        else:
            x = jnp.concatenate([a_ref[...], b_ref[...]], axis=1)
            y = pltpu.roll(x, width - shift, 1)[:, :REPACK_COLS]
        o_ref[...] = y.astype(BF16)

    @pl.when(t < first_after_mi)
    def _():
        emit(0)

    @pl.when((t >= first_after_mi) & (t < first_after_glr))
    def _():
        emit(SHIFT_AFTER_MI)

    @pl.when(t >= first_after_glr)
    def _():
        emit(SHIFT_AFTER_GLR)


def _repack_w_in(w_in):
    per_tile = REPACK_COLS // LANE
    return pl.pallas_call(
        _repack_kernel,
        grid=(DEPTH, Z_BIG // REPACK_COLS),
        in_specs=[
            pl.BlockSpec((None, D_MODEL, REPACK_COLS), lambda l, t: (l, 0, t)),
            pl.BlockSpec((None, D_MODEL, LANE), lambda l, t: (l, 0, per_tile * (t + 1))),
        ],
        out_specs=pl.BlockSpec((None, D_MODEL, REPACK_COLS), lambda l, t: (l, 0, t)),
        out_shape=jax.ShapeDtypeStruct((DEPTH, D_MODEL, Z_BIG), BF16),
        compiler_params=pltpu.CompilerParams(
            dimension_semantics=("arbitrary", "arbitrary"),
            vmem_limit_bytes=VMEM_LIMIT_BYTES,
        ),
        name="repack_w_in",
    )(w_in, w_in)


def _narrow_kernel(mi_ref, glr_ref, o_ref):
    lane = lax.broadcasted_iota(jnp.int32, o_ref.shape, 1)
    y = jnp.where(lane < SHIFT_AFTER_MI, mi_ref[...],
                  jnp.where(lane < SHIFT_AFTER_GLR, glr_ref[...], 0.0))
    o_ref[...] = y.astype(BF16)


def _narrow_w_in(w_in):
    assert SRC_MI % LANE == 0 and SRC_GLR % LANE == LR_LANE
    return pl.pallas_call(
        _narrow_kernel,
        grid=(DEPTH,),
        in_specs=[
            pl.BlockSpec((None, D_MODEL, LANE), lambda l: (l, 0, SRC_MI // LANE)),
            pl.BlockSpec((None, D_MODEL, LANE), lambda l: (l, 0, SRC_GLR // LANE)),
        ],
        out_specs=pl.BlockSpec((None, D_MODEL, LANE), lambda l: (l, 0, 0)),
        out_shape=jax.ShapeDtypeStruct((DEPTH, D_MODEL, Z_SMALL), BF16),
        compiler_params=pltpu.CompilerParams(dimension_semantics=("arbitrary",)),
        name="narrow_w_in",
    )(w_in, w_in)


N_PROJ_STAGES = 1 + Z_BIG // PROJ_COLS


def _project_stages(h_ref, r0, rows, g_ref, wbig_ref, wsm_ref, z_ref, zs_ref):
    held = {}

    def head():
        a = _rmsnorm(h_ref[r0:r0 + rows, :], g_ref[...]).astype(BF16)
        held["a"] = a
        zs_ref[...] = _dot(a, wsm_ref[...])

    def body(n):
        cols = slice(n * PROJ_COLS, (n + 1) * PROJ_COLS)
        z_ref[:, cols] = _dot(held["a"], wbig_ref[:, cols]).astype(BF16)

    return [head] + [functools.partial(body, n) for n in range(Z_BIG // PROJ_COLS)]


def _anchor(x, half):
    rows, width = x.shape
    if half >= SUBLANE:
        parts = []
        for blk in range(rows // (2 * half)):
            r = blk * 2 * half + half - 1
            parts.append(jnp.broadcast_to(x[r:r + 1, :], (2 * half, width)))
        return parts[0] if len(parts) == 1 else jnp.concatenate(parts, axis=0)
    x3 = x.reshape(rows // SUBLANE, SUBLANE, width)
    sub = lax.broadcasted_iota(jnp.int32, x3.shape, 1)

    def bc(i):
        return jnp.broadcast_to(x3[:, i:i + 1, :], x3.shape)

    if half == 4:
        out = bc(3)
    elif half == 2:
        out = jnp.where(sub < 4, bc(1), bc(5))
    else:
        out = jnp.where(sub < 2, bc(0), jnp.where(sub < 4, bc(2), jnp.where(sub < 6, bc(4), bc(6))))
    return out.reshape(rows, width)


def _mix_chunk(zb_ref, zs_ref, r0, state_in, params, outs, *, L, seg, fillers=()):
    cin, nin, min_, sin = state_in
    bias_ref, w2_ref, blr_ref, gm_ref, gg_ref = params
    hm_ref, hg_ref, zg_ref, c_ref, n_ref, m_ref, s_ref = outs
    orow = slice(r0, r0 + L)
    nseg = L // seg
    seg_shift = seg.bit_length() - 1

    row = lax.broadcasted_iota(jnp.int32, (L, L), 0)
    col = lax.broadcasted_iota(jnp.int32, (L, L), 1)
    rxc = row ^ col
    causal = (col <= row) & (rxc < seg)
    tril_b = causal.astype(F32).astype(BF16)
    lane = lax.broadcasted_iota(jnp.int32, (L, LANE), 1)
    rowl = lax.broadcasted_iota(jnp.int32, (L, LANE), 0)
    eye_k = (lax.broadcasted_iota(jnp.int32, (DK, DK), 0)
             == lax.broadcasted_iota(jnp.int32, (DK, DK), 1))
    lane4 = lax.broadcasted_iota(jnp.int32, (1, HEADS), 1)

    def seg_last(x):
        parts = [jnp.broadcast_to(x[(g + 1) * seg - 1:(g + 1) * seg, :], (seg, x.shape[1]))
                 for g in range(nseg)]
        return parts[0] if nseg == 1 else jnp.concatenate(parts, axis=0)

    def seg_rows(vals):
        parts = [jnp.broadcast_to(v, (seg, v.shape[1])) for v in vals]
        return parts[0] if nseg == 1 else jnp.concatenate(parts, axis=0)

    def seg_mask(x, g):
        return x if nseg == 1 else jnp.where((rowl >> seg_shift) == g, x, 0.0)

    def per_seg_dot(x_f32, state_ref, h):
        parts = [_dot(x_f32[g * seg:(g + 1) * seg, :].astype(BF16), state_ref[g, h].astype(BF16))
                 for g in range(nseg)]
        return parts[0] if nseg == 1 else jnp.concatenate(parts, axis=0)

    fillers = list(fillers)

    def next_filler():
        if fillers:
            fillers.pop(0)()

    next_filler()
    zg_ref[orow, :] = zb_ref[:, Z_MIX:]

    zs = zs_ref[...]
    gates = zs + bias_ref[...]
    lf = _log_sigmoid(gates)
    cum = _masked_cumsum(tril_b, lf, 3)
    cols = jnp.where(lane < HEADS, gates, cum)
    sel = (lax.broadcasted_iota(jnp.int32, (SUBLANE, LANE), 0)
           == lax.broadcasted_iota(jnp.int32, (SUBLANE, LANE), 1)).astype(F32)
    rows_ = lax.dot_general(sel, cols, (((1,), (1,)), ((), ())),
                            precision=HIGHEST, preferred_element_type=F32)

    m_old = [min_[g] for g in range(nseg)]
    m_new = [jnp.zeros((1, HEADS), F32) for _ in range(nseg)]

    for h in range(HEADS):
        q = zb_ref[:, MQ + DK * h:MQ + DK * (h + 1)]
        k = zb_ref[:, MK + DK * h:MK + DK * (h + 1)]
        v = zb_ref[:, MV + DV * h:MV + DV * (h + 1)]
        qf = q.astype(F32)
        li_c = cols[:, h:h + 1]
        b_c = cols[:, HEADS + h:HEADS + h + 1]
        li_r = rows_[h:h + 1, :]
        b_r = rows_[HEADS + h:HEADS + h + 1, :]
        m_col = seg_rows([m_old[g][:, h:h + 1] for g in range(nseg)])
        dmat = jnp.where(causal, b_c + (li_r - b_r), -jnp.inf)
        inter = b_c + m_col
        mt = jnp.maximum(inter, jnp.max(dmat, axis=1, keepdims=True))
        w_inter = jnp.exp(inter - mt)
        a = jnp.exp(dmat - mt) * (_dot_nt(q, k) * SCALE_K)
        n_rows = seg_rows([nin[g, h:h + 1, :] for g in range(nseg)])
        qn = jnp.sum(qf * n_rows, axis=1, keepdims=True)
        num = _dot(a.astype(BF16), v) + w_inter * per_seg_dot(qf, cin, h)
        den = jnp.sum(a, axis=1, keepdims=True) + w_inter * qn
        den = jnp.maximum(jnp.abs(den), jnp.exp(-mt))
        hh = num * (1.0 / den)
        mu = jnp.mean(hh, axis=-1, keepdims=True)
        hc = hh - mu
        hn = hc * lax.rsqrt(jnp.mean(hc * hc, axis=-1, keepdims=True) + EPS)
        og = zb_ref[:, MO + DV * h:MO + DV * (h + 1)].astype(F32)
        hm_ref[orow, DV * h:DV * (h + 1)] = (hn * gm_ref[:, DV * h:DV * (h + 1)] * _sigmoid(og)).astype(BF16)

        ws = jnp.exp(seg_last(b_c) - b_c + li_c - seg_last(mt)) * SCALE_K
        kw = k.astype(F32) * ws
        for g in range(nseg):
            last = (g + 1) * seg - 1
            m_last = mt[last:last + 1, :]
            cw = jnp.exp(b_c[last:last + 1, :] + m_old[g][:, h:h + 1] - m_last)
            c_new = cw * cin[g, h] + _dot_tn(seg_mask(kw, g).astype(BF16), v)
            n_new = cw * nin[g, h:h + 1, :] + jnp.sum(kw[g * seg:(g + 1) * seg, :], axis=0, keepdims=True)
            c_ref[g, h] = c_new
            n_ref[g, h:h + 1, :] = n_new
            m_new[g] = jnp.where(lane4 == h, m_last, m_new[g])
        next_filler()
    for g in range(nseg):
        m_ref[g] = m_new[g]

    la = _log_sigmoid(_dot(zs.astype(BF16), w2_ref[...]) + blr_ref[...]) * (1.0 / G_TAU)
    bc = _masked_cumsum(tril_b, la, 2)
    qs = [zb_ref[:, GQ + DK * h:GQ + DK * (h + 1)].astype(F32) for h in range(HEADS)]
    ks = [zb_ref[:, GK + DK * h:GK + DK * (h + 1)].astype(F32) for h in range(HEADS)]
    qe = [qs[h] * jnp.exp(bc[:, DK * h:DK * (h + 1)]) for h in range(HEADS)]

    def intra_single_anchor():
        out = []
        for h in range(HEADS):
            ke = ks[h] * jnp.exp(-bc[:, DK * h:DK * (h + 1)])
            out.append(jnp.where(causal, _dot_nt(qe[h].astype(BF16), ke.astype(BF16)), 0.0))
        return out

    def intra_by_levels():
        out = [jnp.where(row == col, jnp.sum(qs[h] * ks[h], axis=1, keepdims=True), 0.0)
               for h in range(HEADS)]
        half = 1
        while half < seg:
            wgt = jnp.exp(-jnp.abs(bc - _anchor(bc, half)))
            pair = (col < row) & (rxc >= half) & (rxc < 2 * half)
            for h in range(HEADS):
                w_h = wgt[:, DK * h:DK * (h + 1)]
                lvl = _dot_nt((qs[h] * w_h).astype(BF16), (ks[h] * w_h).astype(BF16))
                out[h] = jnp.where(pair, lvl, out[h])
            half *= 2
        return out

    amat = lax.cond(jnp.min(bc) >= GLA_SAFE_LOG_DECAY, intra_single_anchor, intra_by_levels)

    bl_rows = seg_last(bc)
    for h in range(HEADS):
        v = zb_ref[:, GV + DV * h:GV + DV * (h + 1)]
        b_h = bc[:, DK * h:DK * (h + 1)]
        o = _dot((amat[h] * SCALE_K).astype(BF16), v) + SCALE_K * per_seg_dot(qe[h], sin, h)
        on = o * lax.rsqrt(jnp.mean(o * o, axis=-1, keepdims=True) + EPS)
        gg = zb_ref[:, GG + DV * h:GG + DV * (h + 1)].astype(F32)
        hg_ref[orow, DV * h:DV * (h + 1)] = (on * gg_ref[:, DV * h:DV * (h + 1)] * (gg * _sigmoid(gg))).astype(BF16)

        kt = ks[h] * jnp.exp(bl_rows[:, DK * h:DK * (h + 1)] - b_h)
        for g in range(nseg):
            last = (g + 1) * seg - 1
            dec_row = jnp.exp(b_h[last:last + 1, :])
            dec_col = jnp.sum(jnp.where(eye_k, jnp.broadcast_to(dec_row, (DK, DK)), 0.0),
                              axis=1, keepdims=True)
            s_ref[g, h] = dec_col * sin[g, h] + _dot_tn(seg_mask(kt, g).astype(BF16), v)
        next_filler()
    assert not fillers


def _front_kernel(*refs, L, seg, pipelined, aliased):
    n_h = 2 if pipelined else 1
    h_refs = refs[:n_h]
    pos = n_h
    g_ref, wbig_ref, wsm_ref = refs[pos:pos + 3]
    pos += 3
    c0_ref, n0_ref, m0_ref, s0_ref = refs[pos:pos + 4]
    pos += 4
    params = refs[pos:pos + 5]
    pos += 5 + (4 if aliased else 0)
    outs = refs[pos:pos + 7]
    pos += 7
    scratch = refs[pos:]
    c_ref, n_ref, m_ref, s_ref = outs[3:]
    stages = functools.partial(_project_stages, g_ref=g_ref, wbig_ref=wbig_ref, wsm_ref=wsm_ref)

    if not pipelined:
        z_a, zs_a = scratch
        for stage in stages(h_refs[0], 0, L, z_ref=z_a, zs_ref=zs_a):
            stage()
        _mix_chunk(z_a, zs_a, 0, (c0_ref, n0_ref, m0_ref, s0_ref), params, outs, L=L, seg=seg)
        return

    h_cur, h_nxt = h_refs
    z_a, zs_a, z_b, zs_b = scratch

    @pl.when(pl.program_id(1) == 0)
    def _():
        c_ref[...] = c0_ref[...]
        n_ref[...] = n0_ref[...]
        m_ref[...] = m0_ref[...]
        s_ref[...] = s0_ref[...]
        for stage in stages(h_cur, 0, L, z_ref=z_a, zs_ref=zs_a):
            stage()

    carried = (c_ref, n_ref, m_ref, s_ref)
    _mix_chunk(z_a, zs_a, 0, carried, params, outs, L=L, seg=seg,
               fillers=stages(h_cur, L, L, z_ref=z_b, zs_ref=zs_b))
    _mix_chunk(z_b, zs_b, L, carried, params, outs, L=L, seg=seg,
               fillers=stages(h_nxt, 0, L, z_ref=z_a, zs_ref=zs_a))


def _front(h, states_in, in_layer, states_out, layer, W, *, L, seg, seq_rows):
    T = h.shape[0]
    nseg = L // seg
    B = states_in[0].shape[1]
    pipelined = seq_rows > L
    rows = 2 * L if pipelined else L
    steps = seq_rows // rows if pipelined else 1
    grid = (B // nseg, steps)
    row_map = lambda b, k: (b * steps + k, 0)
    nxt_map = lambda b, k: (b * steps + jnp.minimum(k + 1, steps - 1), 0)
    par = lambda b, k: (layer, 0, 0)

    def state_specs(at):
        return [
            pl.BlockSpec((None, nseg, HEADS, DK, DV), lambda b, k: (at, b, 0, 0, 0)),
            pl.BlockSpec((None, nseg, HEADS, DK), lambda b, k: (at, b, 0, 0)),
            pl.BlockSpec((None, nseg, 1, HEADS), lambda b, k: (at, b, 0, 0)),
            pl.BlockSpec((None, nseg, HEADS, DK, DV), lambda b, k: (at, b, 0, 0, 0)),
        ]

    in_specs = [pl.BlockSpec((rows, D_MODEL), row_map)]
    args = [h]
    if pipelined:
        in_specs.append(pl.BlockSpec((rows, D_MODEL), nxt_map))
        args.append(h)
    in_specs += [
        _resident((None, 1, D_MODEL), par),
        _resident((None, D_MODEL, Z_BIG), par),
        _resident((None, D_MODEL, Z_SMALL), par),
        *state_specs(in_layer),
        _resident((None, 1, Z_SMALL), par),
        _resident((None, Z_SMALL, HEADS * DK), par),
        _resident((None, 1, HEADS * DK), par),
        _resident((None, 1, HEADS * DV), par),
        _resident((None, 1, HEADS * DV), par),
    ]
    args += [W["norm_mix_g"], W["w_big"], W["w_small"], *states_in,
             W["gate_bias"], W["w_lr2"], W["b_lr2"], W["g_mnorm"], W["g_gnorm"]]
    aliases = {}
    if states_out is not None:
        for n, s in enumerate(states_out):
            aliases[len(args)] = 3 + n
            in_specs.append(pl.BlockSpec(memory_space=pl.ANY))
            args.append(s)
    out_shape = [
        jax.ShapeDtypeStruct((T, HEADS * DV), BF16),
        jax.ShapeDtypeStruct((T, HEADS * DV), BF16),
        jax.ShapeDtypeStruct((T, Z_GATES), BF16),
        jax.ShapeDtypeStruct((DEPTH, B, HEADS, DK, DV), F32),
        jax.ShapeDtypeStruct((DEPTH, B, HEADS, DK), F32),
        jax.ShapeDtypeStruct((DEPTH, B, 1, HEADS), F32),
        jax.ShapeDtypeStruct((DEPTH, B, HEADS, DK, DV), F32),
    ]
    z_scratch = [pltpu.VMEM((L, Z_BIG), BF16), pltpu.VMEM((L, Z_SMALL), F32)]
    outs = pl.pallas_call(
        functools.partial(_front_kernel, L=L, seg=seg, pipelined=pipelined, aliased=states_out is not None),
        grid=grid,
        in_specs=in_specs,
        out_specs=[
            pl.BlockSpec((rows, HEADS * DV), row_map),
            pl.BlockSpec((rows, HEADS * DV), row_map),
            pl.BlockSpec((rows, Z_GATES), row_map),
            *state_specs(layer),
        ],
        out_shape=out_shape,
        input_output_aliases=aliases,
        scratch_shapes=z_scratch * (2 if pipelined else 1),
        compiler_params=pltpu.CompilerParams(
            dimension_semantics=("arbitrary", "arbitrary"),
            vmem_limit_bytes=VMEM_LIMIT_BYTES,
        ),
        name="front",
    )(*args)
    return outs[0], outs[1], outs[2], tuple(outs[3:])


def _channel_kernel(*refs, seq_tiles, has_state, final):
    (h_ref, hm_ref, hg_ref, ga_ref, gb_ref, wpa_ref, wpb_ref, wo_ref, gffn_ref,
     wup_ref, cw_ref, cb_ref, wd_ref, gple_ref, wpg_ref, wpp_ref, p_ref) = refs[:17]
    pos = 17
    pst_ref = gfin_ref = carry_scr = None
    if has_state:
        pst_ref = refs[pos]
        pos += 1
    if final:
        gfin_ref = refs[pos]
        pos += 1
    out_ref, conv_ref = refs[pos:pos + 2]
    if not has_state:
        carry_scr = refs[pos + 2]

    i = pl.program_id(0)
    tm = h_ref.shape[0]

    pa = _dot(hm_ref[...], wpa_ref[...])
    pb = _dot(hg_ref[...], wpb_ref[...])
    merged = _sigmoid(ga_ref[...].astype(F32)) * pa + _sigmoid(gb_ref[...].astype(F32)) * pb
    h1 = h_ref[...] + _dot(merged.astype(BF16), wo_ref[...])
    c = _rmsnorm(h1, gffn_ref[...]).astype(BF16)

    up = [_dot(c, wup_ref[:, n * UP_CHUNK:(n + 1) * UP_CHUNK]) for n in range(2 * D_FF // UP_CHUNK)]

    def up_cols(start):
        n, off = divmod(start, UP_CHUNK)
        assert off + FF_CHUNK <= UP_CHUNK
        return up[n][:, off:off + FF_CHUNK]

    rowi = lax.broadcasted_iota(jnp.int32, (tm, FF_CHUNK), 0)
    acc = None
    for n in range(D_FF // FF_CHUNK):
        ff = slice(n * FF_CHUNK, (n + 1) * FF_CHUNK)
        ug = up_cols(n * FF_CHUNK)
        uv = up_cols(D_FF + n * FF_CHUNK)
        r1 = pltpu.roll(ug, 1, 0)
        r2 = pltpu.roll(ug, 2, 0)
        if has_state:
            pst = pst_ref[:, ff]
            t = rowi & (SUBLANE - 1)
            u1 = jnp.where(t == 0, pltpu.roll(pst, tm - 1, 0), r1)
            u2 = jnp.where(t < 2, pst, r2)
            conv_ref[:, ff] = ug
        else:
            prev = jnp.where(i % seq_tiles == 0, 0.0, carry_scr[:, ff])
            p6 = jnp.broadcast_to(prev[6:7, :], ug.shape)
            p7 = jnp.broadcast_to(prev[7:8, :], ug.shape)
            u1 = jnp.where(rowi == 0, p7, r1)
            u2 = jnp.where(rowi == 0, p6, jnp.where(rowi == 1, p7, r2))
            tail = ug[tm - SUBLANE:, :]
            carry_scr[:, ff] = tail
            conv_ref[:, ff] = tail
        conv = cb_ref[:, ff] + cw_ref[0:1, ff] * u2 + cw_ref[1:2, ff] * u1 + cw_ref[2:3, ff] * ug
        act = conv * _sigmoid(conv) * uv
        part = _dot(act.astype(BF16), wd_ref[ff, :])
        acc = part if acc is None else acc + part

    h2 = h1 + acc
    e = _rmsnorm(h2, gple_ref[...]).astype(BF16)
    h3 = h2 + _sigmoid(_dot(e, wpg_ref[...])) * _dot(p_ref[...].astype(BF16), wpp_ref[...])
    if final:
        h3 = _rmsnorm(h3, gfin_ref[...])
    out_ref[...] = h3


def _channel(h, hm, hg, zg, p, pst, layer, W, *, seq_rows, final):
    T = h.shape[0]
    tm = CHAN_ROWS
    has_state = pst is not None
    seq_tiles = max(seq_rows // tm, 1)
    rows = lambda i: (i, 0)
    par = lambda i: (layer, 0, 0)
    in_specs = [
        pl.BlockSpec((tm, D_MODEL), rows),
        pl.BlockSpec((tm, D_MODEL), rows),
        pl.BlockSpec((tm, D_MODEL), rows),
        pl.BlockSpec((tm, D_MODEL), lambda i: (i, (GA - Z_MIX) // D_MODEL)),
        pl.BlockSpec((tm, D_MODEL), lambda i: (i, (GB - Z_MIX) // D_MODEL)),
        _resident((None, D_MODEL, D_MODEL), par),
        _resident((None, D_MODEL, D_MODEL), par),
        _resident((None, D_MODEL, D_MODEL), par),
        _resident((None, 1, D_MODEL), par),
        _resident((None, D_MODEL, 2 * D_FF), par),
        _resident((None, CONV_W, D_FF), par),
        _resident((None, 1, D_FF), par),
        _resident((None, D_FF, D_MODEL), par),
        _resident((None, 1, D_MODEL), par),
        _resident((None, D_MODEL, D_MODEL), par),
        _resident((None, PLE_DIM, D_MODEL), par),
        pl.BlockSpec((None, tm, PLE_DIM), lambda i: (layer, i, 0)),
    ]
    args = [h, hm, hg, zg, zg, W["w_pa"], W["w_pb"], W["w_o"], W["norm_ffn_g"],
            W["w_up"], W["conv_w"], W["conv_b"], W["w_down"], W["norm_ple_g"],
            W["w_ple_gate"], W["w_ple_proj"], p]
    scratch = []
    if has_state:
        in_specs.append(pl.BlockSpec((tm, D_FF), rows))
        args.append(pst)
        conv_spec = pl.BlockSpec((tm, D_FF), rows)
        conv_shape = jax.ShapeDtypeStruct((T, D_FF), F32)
    else:
        conv_spec = pl.BlockSpec((None, SUBLANE, D_FF), lambda i: (i, 0, 0))
        conv_shape = jax.ShapeDtypeStruct((T // tm, SUBLANE, D_FF), F32)
        scratch.append(pltpu.VMEM((SUBLANE, D_FF), F32))
    if final:
        in_specs.append(_resident((1, D_MODEL), lambda i: (0, 0)))
        args.append(W["final_norm_g"])
    return pl.pallas_call(
        functools.partial(_channel_kernel, seq_tiles=seq_tiles, has_state=has_state, final=final),
        grid=(T // tm,),
        in_specs=in_specs,
        out_specs=[pl.BlockSpec((tm, D_MODEL), rows), conv_spec],
        out_shape=[jax.ShapeDtypeStruct((T, D_MODEL), F32), conv_shape],
        scratch_shapes=scratch,
        compiler_params=pltpu.CompilerParams(
            dimension_semantics=("arbitrary",),
            vmem_limit_bytes=VMEM_LIMIT_BYTES,
        ),
        name="channel",
    )(*args)


def _prep_weights(norm_mix_g, w_in, b_mi, b_mf, g_mnorm, w_lr2, b_lr2, g_gnorm, w_pa, w_pb, w_o,
                  norm_ffn_g, w_up, conv_w, conv_b, w_down, norm_ple_g, w_ple_gate, w_ple_proj,
                  final_norm_g):
    gate_bias = jnp.concatenate([b_mi, b_mf, jnp.zeros((DEPTH, Z_SMALL - 2 * HEADS), F32)], axis=-1)
    w_lr2_pad = jnp.zeros((DEPTH, Z_SMALL, HEADS * DK), F32).at[:, LR_LANE:LR_LANE + G_RANK, :].set(w_lr2)
    row = lambda x: x.reshape(DEPTH, 1, x.shape[-1])
    return dict(
        norm_mix_g=row(norm_mix_g), w_big=_repack_w_in(w_in), w_small=_narrow_w_in(w_in),
        gate_bias=row(gate_bias), w_lr2=w_lr2_pad.astype(BF16), b_lr2=row(b_lr2),
        g_mnorm=row(g_mnorm), g_gnorm=row(g_gnorm),
        w_pa=w_pa.astype(BF16), w_pb=w_pb.astype(BF16), w_o=w_o.astype(BF16),
        norm_ffn_g=row(norm_ffn_g), w_up=w_up.astype(BF16), conv_w=conv_w, conv_b=row(conv_b),
        w_down=w_down.astype(BF16), norm_ple_g=row(norm_ple_g),
        w_ple_gate=w_ple_gate.astype(BF16), w_ple_proj=w_ple_proj.astype(BF16),
        final_norm_g=final_norm_g.reshape(1, D_MODEL),
    )


def _trunk(x, p, states, stacked, cv0, W, *, seq_rows):
    T = x.shape[0]
    B = T // seq_rows
    if seq_rows >= 2 * PROMPT_CHUNK:
        L, seg = PROMPT_CHUNK, PROMPT_CHUNK
    else:
        L, seg = SAMPLE_GROUP * seq_rows, seq_rows
    h = x
    new_states = None
    cvs = []
    for layer in range(DEPTH):
        hm, hg, zg, new_states = _front(h, states, layer if stacked else 0, new_states, layer, W,
                                        L=L, seg=seg, seq_rows=seq_rows)
        if cv0 is None:
            pst = None
        else:
            pst = jnp.pad(cv0[layer], ((0, 0), (0, seq_rows - (CONV_W - 1)), (0, 0))).reshape(T, D_FF)
        h, cv = _channel(h, hm, hg, zg, p, pst, layer, W, seq_rows=seq_rows, final=layer == DEPTH - 1)
        if cv0 is None:
            cv = cv.reshape(B, -1, SUBLANE, D_FF)[:, -1, SUBLANE - (CONV_W - 1):, :]
        else:
            cv = cv.reshape(B, seq_rows, D_FF)[:, seq_rows - (CONV_W - 1):, :]
        cvs.append(cv)
    c, n, m, s = new_states
    return h, c, n, m.reshape(DEPTH, B, HEADS), s, jnp.stack(cvs)


def kernel(x_prompt, x_sample, state_mlstm_C, state_mlstm_n, state_mlstm_m, state_gla_S, state_ffn_conv, p_prompt, p_sample, norm_mix_g, w_in, b_mi, b_mf, g_mnorm, w_lr2, b_lr2, g_gnorm, w_pa, w_pb, w_o, norm_ffn_g, w_up, conv_w, conv_b, w_down, norm_ple_g, w_ple_gate, w_ple_proj, final_norm_g):
    W = _prep_weights(norm_mix_g, w_in, b_mi, b_mf, g_mnorm, w_lr2, b_lr2, g_gnorm, w_pa, w_pb, w_o,
                      norm_ffn_g, w_up, conv_w, conv_b, w_down, norm_ple_g, w_ple_gate, w_ple_proj,
                      final_norm_g)
    bp, sp, _ = x_prompt.shape
    bs, ss, _ = x_sample.shape
    zero_states = (jnp.zeros((1, bp, HEADS, DK, DV), F32), jnp.zeros((1, bp, HEADS, DK), F32),
                   jnp.zeros((1, bp, 1, HEADS), F32), jnp.zeros((1, bp, HEADS, DK, DV), F32))
    yp, c_p, n_p, m_p, s_p, cv_p = _trunk(
        x_prompt.reshape(bp * sp, D_MODEL), p_prompt.reshape(DEPTH, bp * sp, PLE_DIM),
        zero_states, False, None, W, seq_rows=sp)
    sample_states = (state_mlstm_C, state_mlstm_n, state_mlstm_m.reshape(DEPTH, bs, 1, HEADS), state_gla_S)
    ys, c_s, n_s, m_s, s_s, cv_s = _trunk(
        x_sample.reshape(bs * ss, D_MODEL), p_sample.reshape(DEPTH, bs * ss, PLE_DIM),
        sample_states, True, state_ffn_conv, W, seq_rows=ss)
    return (yp.reshape(bp, sp, D_MODEL), ys.reshape(bs, ss, D_MODEL),
            c_p, n_p, m_p, s_p, cv_p, c_s, n_s, m_s, s_s, cv_s)
```

```python
import functools

import jax
import jax.numpy as jnp
from jax import lax
from jax.experimental import pallas as pl
from jax.experimental.pallas import tpu as pltpu

F32 = jnp.float32
BF16 = jnp.bfloat16
HIGHEST = lax.Precision.HIGHEST

D_MODEL = 1024
DEPTH = 4
HEADS = 4
DK = 128
DV = 256
G_RANK = 16
G_TAU = 16.0
D_FF = 2688
CONV_W = 3
PLE_DIM = 256
EPS = 1e-6
SCALE_K = DK ** -0.5

MQ, MK, MV, MO = 0, 512, 1024, 2048
GQ, GK, GV, GG = 3072, 3584, 4096, 5120
GA, GB = 6144, 7168
Z_BIG = 8192
Z_MIX = 6144
Z_GATES = Z_BIG - Z_MIX
Z_SMALL = 128
LR_LANE = 8
SRC_MI, SRC_GLR = 2048, 5128
SHIFT_AFTER_MI = 2 * HEADS
SHIFT_AFTER_GLR = 2 * HEADS + G_RANK

LANE = 128
SUBLANE = 8
VMEM_LIMIT_BYTES = 58 * 1024 * 1024

PROJ_ROWS = 512
PROJ_COLS = 256
CHAN_ROWS = 256
UP_CHUNK = 1792
FF_CHUNK = 896
REPACK_COLS = 512
PROMPT_CHUNK = 128
SAMPLE_GROUP = 4
GLA_SAFE_LOG_DECAY = -60.0


def _dot(a, b):
    return jnp.dot(a, b, preferred_element_type=F32)


def _dot_nt(a, b):
    return lax.dot_general(a, b, (((1,), (1,)), ((), ())), preferred_element_type=F32)


def _dot_tn(a, b):
    return lax.dot_general(a, b, (((0,), (0,)), ((), ())), preferred_element_type=F32)


def _log_sigmoid(x):
    return jnp.minimum(x, 0.0) - jnp.log(1.0 + jnp.exp(-jnp.abs(x)))


def _masked_cumsum(mask_bf16, x, pieces):
    acc = None
    rest = x
    for n in range(pieces):
        piece = rest.astype(BF16)
        part = _dot(mask_bf16, piece)
        acc = part if acc is None else acc + part
        if n + 1 < pieces:
            rest = rest - piece.astype(F32)
    return acc


def _sigmoid(x):
    return 0.5 * jnp.tanh(0.5 * x) + 0.5


def _rmsnorm(x, g):
    return x * lax.rsqrt(jnp.mean(x * x, axis=-1, keepdims=True) + EPS) * g


def _resident(block_shape, index_map):
    return pl.BlockSpec(block_shape, index_map, pipeline_mode=pl.Buffered(1))


REPACK_TAIL = 32


def _repack_kernel(a_ref, b_ref, o_ref):
    t = pl.program_id(1)
    first_after_mi = MO // REPACK_COLS
    first_after_glr = GG // REPACK_COLS

    def emit(shift):
        if shift == 0:
            y = a_ref[...]
        else:
            x = jnp.concatenate([a_ref[...], b_ref[...]], axis=0)
            y = x[shift:shift + REPACK_COLS, :]
        o_ref[...] = y.T.astype(BF16)

    @pl.when(t < first_after_mi)
    def _():
        emit(0)

    @pl.when((t >= first_after_mi) & (t < first_after_glr))
    def _():
        emit(SHIFT_AFTER_MI)

    @pl.when(t >= first_after_glr)
    def _():
        emit(SHIFT_AFTER_GLR)


def _repack_w_in(w_in_t):
    tails_per_tile = REPACK_COLS // REPACK_TAIL
    return pl.pallas_call(
        _repack_kernel,
        grid=(DEPTH, Z_BIG // REPACK_COLS),
        in_specs=[
            pl.BlockSpec((None, REPACK_COLS, D_MODEL), lambda l, t: (l, t, 0)),
            pl.BlockSpec((None, REPACK_TAIL, D_MODEL), lambda l, t: (l, tails_per_tile * (t + 1), 0)),
        ],
        out_specs=pl.BlockSpec((None, D_MODEL, REPACK_COLS), lambda l, t: (l, 0, t)),
        out_shape=jax.ShapeDtypeStruct((DEPTH, D_MODEL, Z_BIG), BF16),
        compiler_params=pltpu.CompilerParams(
            dimension_semantics=("arbitrary", "arbitrary"),
            vmem_limit_bytes=VMEM_LIMIT_BYTES,
        ),
        name="repack_w_in",
    )(w_in_t, w_in_t)


def _narrow_kernel(mi_ref, glr0_ref, glr1_ref, o_ref):
    pad = jnp.zeros((Z_SMALL - SHIFT_AFTER_GLR, D_MODEL), F32)
    x = jnp.concatenate([mi_ref[...], glr0_ref[...], glr1_ref[...], pad], axis=0)
    o_ref[...] = x.T.astype(BF16)


def _narrow_w_in(w_in_t):
    assert SRC_MI % SUBLANE == 0 and SRC_GLR % SUBLANE == 0 and G_RANK == 2 * SUBLANE
    rows8 = lambda r: pl.BlockSpec((None, SUBLANE, D_MODEL), lambda l: (l, r // SUBLANE, 0))
    return pl.pallas_call(
        _narrow_kernel,
        grid=(DEPTH,),
        in_specs=[rows8(SRC_MI), rows8(SRC_GLR), rows8(SRC_GLR + SUBLANE)],
        out_specs=pl.BlockSpec((None, D_MODEL, Z_SMALL), lambda l: (l, 0, 0)),
        out_shape=jax.ShapeDtypeStruct((DEPTH, D_MODEL, Z_SMALL), BF16),
        compiler_params=pltpu.CompilerParams(dimension_semantics=("arbitrary",)),
        name="narrow_w_in",
    )(w_in_t, w_in_t, w_in_t)


def _in_proj_kernel(h_ref, g_ref, wbig_ref, wsm_ref, zbig_ref, zsm_ref):
    a = _rmsnorm(h_ref[...], g_ref[...]).astype(BF16)
    zsm_ref[...] = _dot(a, wsm_ref[...])
    for n in range(Z_BIG // PROJ_COLS):
        cols = slice(n * PROJ_COLS, (n + 1) * PROJ_COLS)
        zbig_ref[:, cols] = _dot(a, wbig_ref[:, cols]).astype(BF16)


def _in_proj(h, layer, W):
    T = h.shape[0]
    par = lambda i: (layer, 0, 0)
    return pl.pallas_call(
        _in_proj_kernel,
        grid=(T // PROJ_ROWS,),
        in_specs=[
            pl.BlockSpec((PROJ_ROWS, D_MODEL), lambda i: (i, 0)),
            _resident((None, 1, D_MODEL), par),
            _resident((None, D_MODEL, Z_BIG), par),
            _resident((None, D_MODEL, Z_SMALL), par),
        ],
        out_specs=[
            pl.BlockSpec((PROJ_ROWS, Z_BIG), lambda i: (i, 0)),
            pl.BlockSpec((PROJ_ROWS, Z_SMALL), lambda i: (i, 0)),
        ],
        out_shape=[
            jax.ShapeDtypeStruct((T, Z_BIG), BF16),
            jax.ShapeDtypeStruct((T, Z_SMALL), F32),
        ],
        compiler_params=pltpu.CompilerParams(
            dimension_semantics=("arbitrary",),
            vmem_limit_bytes=VMEM_LIMIT_BYTES,
        ),
        name="in_proj",
    )(h, W["norm_mix_g"], W["w_big"], W["w_small"])


N_PROJ_STAGES = 1 + Z_BIG // PROJ_COLS
MIX_STAGES = 1 + 2 * HEADS


def _project_stages(h_ref, r0, rows, g_ref, wbig_ref, wsm_ref, z_ref, zs_ref):
    held = {}

    def head():
        a = _rmsnorm(h_ref[r0:r0 + rows, :], g_ref[...]).astype(BF16)
        held["a"] = a
        zs_ref[...] = _dot(a, wsm_ref[...])

    def body(n):
        cols = slice(n * PROJ_COLS, (n + 1) * PROJ_COLS)
        z_ref[:, cols] = _dot(held["a"], wbig_ref[:, cols]).astype(BF16)

    return [head] + [functools.partial(body, n) for n in range(Z_BIG // PROJ_COLS)]


def _anchor(x, half):
    rows, width = x.shape
    if half >= SUBLANE:
        parts = []
        for blk in range(rows // (2 * half)):
            r = blk * 2 * half + half - 1
            parts.append(jnp.broadcast_to(x[r:r + 1, :], (2 * half, width)))
        return parts[0] if len(parts) == 1 else jnp.concatenate(parts, axis=0)
    x3 = x.reshape(rows // SUBLANE, SUBLANE, width)
    sub = lax.broadcasted_iota(jnp.int32, x3.shape, 1)

    def bc(i):
        return jnp.broadcast_to(x3[:, i:i + 1, :], x3.shape)

    if half == 4:
        out = bc(3)
    elif half == 2:
        out = jnp.where(sub < 4, bc(1), bc(5))
    else:
        out = jnp.where(sub < 2, bc(0), jnp.where(sub < 4, bc(2), jnp.where(sub < 6, bc(4), bc(6))))
    return out.reshape(rows, width)


def _causal_mask(L, seg):
    row = lax.broadcasted_iota(jnp.int32, (L, L), 0)
    col = lax.broadcasted_iota(jnp.int32, (L, L), 1)
    return (col <= row) & ((row ^ col) < seg)


def _mix_chunk(zb_ref, zs_ref, r0, state_in, params, outs, *, L, seg, make_fillers=None):
    w2_ref, blr_ref = params[1:3]
    tril_b = _causal_mask(L, seg).astype(F32).astype(BF16)
    la = _log_sigmoid(_dot(zs_ref[...].astype(BF16), w2_ref[...]) + blr_ref[...]) * (1.0 / G_TAU)
    bc = _masked_cumsum(tril_b, la, 2)

    def run(single_anchor):
        fillers = make_fillers() if make_fillers else ()
        _mix_rest(zb_ref, zs_ref, bc, r0, state_in, params, outs, L=L, seg=seg,
                  single_anchor=single_anchor, fillers=fillers)

    lax.cond(jnp.min(bc) >= GLA_SAFE_LOG_DECAY, lambda: run(True), lambda: run(False))


def _mix_rest(zb_ref, zs_ref, bc, r0, state_in, params, outs, *, L, seg, single_anchor, fillers):
    cin, nin, min_, sin = state_in
    bias_ref, w2_ref, blr_ref, gm_ref, gg_ref = params
    hm_ref, hg_ref, zg_ref, c_ref, n_ref, m_ref, s_ref = outs
    orow = slice(r0, r0 + L)
    nseg = L // seg
    seg_shift = seg.bit_length() - 1

    row = lax.broadcasted_iota(jnp.int32, (L, L), 0)
    col = lax.broadcasted_iota(jnp.int32, (L, L), 1)
    rxc = row ^ col
    causal = (col <= row) & (rxc < seg)
    tril_b = causal.astype(F32).astype(BF16)
    lane = lax.broadcasted_iota(jnp.int32, (L, LANE), 1)
    rowl = lax.broadcasted_iota(jnp.int32, (L, LANE), 0)
    eye_k = (lax.broadcasted_iota(jnp.int32, (DK, DK), 0)
             == lax.broadcasted_iota(jnp.int32, (DK, DK), 1))
    lane4 = lax.broadcasted_iota(jnp.int32, (1, HEADS), 1)

    def seg_last(x):
        parts = [jnp.broadcast_to(x[(g + 1) * seg - 1:(g + 1) * seg, :], (seg, x.shape[1]))
                 for g in range(nseg)]
        return parts[0] if nseg == 1 else jnp.concatenate(parts, axis=0)

    def seg_rows(vals):
        parts = [jnp.broadcast_to(v, (seg, v.shape[1])) for v in vals]
        return parts[0] if nseg == 1 else jnp.concatenate(parts, axis=0)

    def seg_mask(x, g):
        return x if nseg == 1 else jnp.where((rowl >> seg_shift) == g, x, 0.0)

    def per_seg_dot(x_f32, state_ref, h):
        parts = [_dot(x_f32[g * seg:(g + 1) * seg, :].astype(BF16), state_ref[g, h].astype(BF16))
                 for g in range(nseg)]
        return parts[0] if nseg == 1 else jnp.concatenate(parts, axis=0)

    fillers = list(fillers)

    per_slot = -(-len(fillers) // MIX_STAGES)

    def next_filler():
        for _ in range(min(per_slot, len(fillers))):
            fillers.pop(0)()

    next_filler()
    zg_ref[orow, :] = zb_ref[:, Z_MIX:]

    zs = zs_ref[...]
    gates = zs + bias_ref[...]
    lf = _log_sigmoid(gates)
    cum = _masked_cumsum(tril_b, lf, 3)
    cols = jnp.where(lane < HEADS, gates, cum)
    sel = (lax.broadcasted_iota(jnp.int32, (SUBLANE, LANE), 0)
           == lax.broadcasted_iota(jnp.int32, (SUBLANE, LANE), 1)).astype(F32)
    rows_ = lax.dot_general(sel, cols, (((1,), (1,)), ((), ())),
                            precision=HIGHEST, preferred_element_type=F32)

    def lhs_with_state(a_intra, x_inter):
        return jnp.concatenate([a_intra.astype(BF16), x_inter.astype(BF16)], axis=1)

    def rhs_with_state(v, state_ref, h):
        return jnp.concatenate([v, state_ref[0, h].astype(BF16)], axis=0)

    m_old = [min_[g] for g in range(nseg)]
    m_new = [jnp.zeros((1, HEADS), F32) for _ in range(nseg)]

    for h in range(HEADS):
        q = zb_ref[:, MQ + DK * h:MQ + DK * (h + 1)]
        k = zb_ref[:, MK + DK * h:MK + DK * (h + 1)]
        v = zb_ref[:, MV + DV * h:MV + DV * (h + 1)]
        qf = q.astype(F32)
        li_c = cols[:, h:h + 1]
        b_c = cols[:, HEADS + h:HEADS + h + 1]
        li_r = rows_[h:h + 1, :]
        b_r = rows_[HEADS + h:HEADS + h + 1, :]
        m_col = seg_rows([m_old[g][:, h:h + 1] for g in range(nseg)])
        dmat = jnp.where(causal, b_c + (li_r - b_r), -jnp.inf)
        inter = b_c + m_col
        mt = jnp.maximum(inter, jnp.max(dmat, axis=1, keepdims=True))
        w_inter = jnp.exp(inter - mt)
        a = jnp.exp(dmat - mt) * (_dot_nt(q, k) * SCALE_K)
        n_rows = seg_rows([nin[g, h:h + 1, :] for g in range(nseg)])
        qn = jnp.sum(qf * n_rows, axis=1, keepdims=True)
        if nseg == 1:
            num = _dot(lhs_with_state(a, w_inter * qf), rhs_with_state(v, cin, h))
        else:
            num = _dot(a.astype(BF16), v) + w_inter * per_seg_dot(qf, cin, h)
        den = jnp.sum(a, axis=1, keepdims=True) + w_inter * qn
        den = jnp.maximum(jnp.abs(den), jnp.exp(-mt))
        hh = num * (1.0 / den)
        mu = jnp.mean(hh, axis=-1, keepdims=True)
        hc = hh - mu
        hn = hc * lax.rsqrt(jnp.mean(hc * hc, axis=-1, keepdims=True) + EPS)
        og = zb_ref[:, MO + DV * h:MO + DV * (h + 1)].astype(F32)
        hm_ref[orow, DV * h:DV * (h + 1)] = (hn * gm_ref[:, DV * h:DV * (h + 1)] * _sigmoid(og)).astype(BF16)

        ws = jnp.exp(seg_last(b_c) - b_c + li_c - seg_last(mt)) * SCALE_K
        kw = k.astype(F32) * ws
        for g in range(nseg):
            last = (g + 1) * seg - 1
            m_last = mt[last:last + 1, :]
            cw = jnp.exp(b_c[last:last + 1, :] + m_old[g][:, h:h + 1] - m_last)
            c_new = cw * cin[g, h] + _dot_tn(seg_mask(kw, g).astype(BF16), v)
            n_new = cw * nin[g, h:h + 1, :] + jnp.sum(kw[g * seg:(g + 1) * seg, :], axis=0, keepdims=True)
            c_ref[g, h] = c_new
            n_ref[g, h:h + 1, :] = n_new
            m_new[g] = jnp.where(lane4 == h, m_last, m_new[g])
        next_filler()
    for g in range(nseg):
        m_ref[g] = m_new[g]

    qs = [zb_ref[:, GQ + DK * h:GQ + DK * (h + 1)].astype(F32) for h in range(HEADS)]
    ks = [zb_ref[:, GK + DK * h:GK + DK * (h + 1)].astype(F32) for h in range(HEADS)]
    qe = [qs[h] * jnp.exp(bc[:, DK * h:DK * (h + 1)]) for h in range(HEADS)]

    def intra_single_anchor():
        out = []
        for h in range(HEADS):
            ke = ks[h] * jnp.exp(-bc[:, DK * h:DK * (h + 1)])
            out.append(jnp.where(causal, _dot_nt(qe[h].astype(BF16), ke.astype(BF16)), 0.0))
        return out

    def intra_by_levels():
        out = [jnp.where(row == col, jnp.sum(qs[h] * ks[h], axis=1, keepdims=True), 0.0)
               for h in range(HEADS)]
        half = 1
        while half < seg:
            wgt = jnp.exp(-jnp.abs(bc - _anchor(bc, half)))
            pair = (col < row) & (rxc >= half) & (rxc < 2 * half)
            for h in range(HEADS):
                w_h = wgt[:, DK * h:DK * (h + 1)]
                lvl = _dot_nt((qs[h] * w_h).astype(BF16), (ks[h] * w_h).astype(BF16))
                out[h] = jnp.where(pair, lvl, out[h])
            half *= 2
        return out

    amat = intra_single_anchor() if single_anchor else intra_by_levels()

    bl_rows = seg_last(bc)
    for h in range(HEADS):
        v = zb_ref[:, GV + DV * h:GV + DV * (h + 1)]
        b_h = bc[:, DK * h:DK * (h + 1)]
        if nseg == 1:
            o = _dot(lhs_with_state(amat[h] * SCALE_K, qe[h] * SCALE_K), rhs_with_state(v, sin, h))
        else:
            o = _dot((amat[h] * SCALE_K).astype(BF16), v) + SCALE_K * per_seg_dot(qe[h], sin, h)
        on = o * lax.rsqrt(jnp.mean(o * o, axis=-1, keepdims=True) + EPS)
        gg = zb_ref[:, GG + DV * h:GG + DV * (h + 1)].astype(F32)
        hg_ref[orow, DV * h:DV * (h + 1)] = (on * gg_ref[:, DV * h:DV * (h + 1)] * (gg * _sigmoid(gg))).astype(BF16)

        kt = ks[h] * jnp.exp(bl_rows[:, DK * h:DK * (h + 1)] - b_h)
        for g in range(nseg):
            last = (g + 1) * seg - 1
            dec_row = jnp.exp(b_h[last:last + 1, :])
            dec_col = jnp.sum(jnp.where(eye_k, jnp.broadcast_to(dec_row, (DK, DK)), 0.0),
                              axis=1, keepdims=True)
            s_ref[g, h] = dec_col * sin[g, h] + _dot_tn(seg_mask(kt, g).astype(BF16), v)
        next_filler()
    assert not fillers


def _front_kernel(*refs, L, seg, pipelined, aliased):
    lead = 5 if pipelined else 2
    pos = lead
    c0_ref, n0_ref, m0_ref, s0_ref = refs[pos:pos + 4]
    pos += 4
    params = refs[pos:pos + 5]
    pos += 5 + (4 if aliased else 0)
    outs = refs[pos:pos + 7]
    pos += 7
    scratch = refs[pos:]
    c_ref, n_ref, m_ref, s_ref = outs[3:]

    if not pipelined:
        z_ref, zs_ref = refs[:lead]
        _mix_chunk(z_ref, zs_ref, 0, (c0_ref, n0_ref, m0_ref, s0_ref), params, outs, L=L, seg=seg)
        return

    h_cur, h_nxt, g_ref, wbig_ref, wsm_ref = refs[:lead]
    stages = functools.partial(_project_stages, g_ref=g_ref, wbig_ref=wbig_ref, wsm_ref=wsm_ref)
    z_a, zs_a, z_b, zs_b = scratch

    @pl.when(pl.program_id(1) == 0)
    def _():
        c_ref[...] = c0_ref[...]
        n_ref[...] = n0_ref[...]
        m_ref[...] = m0_ref[...]
        s_ref[...] = s0_ref[...]
        for stage in stages(h_cur, 0, L, z_ref=z_a, zs_ref=zs_a):
            stage()

    carried = (c_ref, n_ref, m_ref, s_ref)
    _mix_chunk(z_a, zs_a, 0, carried, params, outs, L=L, seg=seg,
               make_fillers=lambda: stages(h_cur, L, L, z_ref=z_b, zs_ref=zs_b))
    _mix_chunk(z_b, zs_b, L, carried, params, outs, L=L, seg=seg,
               make_fillers=lambda: stages(h_nxt, 0, L, z_ref=z_a, zs_ref=zs_a))


def _front(h, states_in, in_layer, states_out, layer, W, *, L, seg, seq_rows):
    T = h.shape[0]
    nseg = L // seg
    B = states_in[0].shape[1]
    pipelined = seq_rows > L
    rows = 2 * L if pipelined else L
    steps = seq_rows // rows if pipelined else 1
    grid = (B // nseg, steps)
    row_map = lambda b, k: (b * steps + k, 0)
    nxt_map = lambda b, k: (b * steps + jnp.minimum(k + 1, steps - 1), 0)
    par = lambda b, k: (layer, 0, 0)

    def state_specs(at):
        return [
            pl.BlockSpec((None, nseg, HEADS, DK, DV), lambda b, k: (at, b, 0, 0, 0)),
            pl.BlockSpec((None, nseg, HEADS, DK), lambda b, k: (at, b, 0, 0)),
            pl.BlockSpec((None, nseg, 1, HEADS), lambda b, k: (at, b, 0, 0)),
            pl.BlockSpec((None, nseg, HEADS, DK, DV), lambda b, k: (at, b, 0, 0, 0)),
        ]

    if pipelined:
        in_specs = [
            pl.BlockSpec((rows, D_MODEL), row_map),
            pl.BlockSpec((rows, D_MODEL), nxt_map),
            _resident((None, 1, D_MODEL), par),
            _resident((None, D_MODEL, Z_BIG), par),
            _resident((None, D_MODEL, Z_SMALL), par),
        ]
        args = [h, h, W["norm_mix_g"], W["w_big"], W["w_small"]]
    else:
        in_specs = [pl.BlockSpec((rows, Z_BIG), row_map), pl.BlockSpec((rows, Z_SMALL), row_map)]
        args = list(_in_proj(h, layer, W))
    in_specs += [
        *state_specs(in_layer),
        _resident((None, 1, Z_SMALL), par),
        _resident((None, Z_SMALL, HEADS * DK), par),
        _resident((None, 1, HEADS * DK), par),
        _resident((None, 1, HEADS * DV), par),
        _resident((None, 1, HEADS * DV), par),
    ]
    args += [*states_in, W["gate_bias"], W["w_lr2"], W["b_lr2"], W["g_mnorm"], W["g_gnorm"]]
    aliases = {}
    if states_out is not None:
        for n, s in enumerate(states_out):
            aliases[len(args)] = 3 + n
            in_specs.append(pl.BlockSpec(memory_space=pl.ANY))
            args.append(s)
    out_shape = [
        jax.ShapeDtypeStruct((T, HEADS * DV), BF16),
        jax.ShapeDtypeStruct((T, HEADS * DV), BF16),
        jax.ShapeDtypeStruct((T, Z_GATES), BF16),
        jax.ShapeDtypeStruct((DEPTH, B, HEADS, DK, DV), F32),
        jax.ShapeDtypeStruct((DEPTH, B, HEADS, DK), F32),
        jax.ShapeDtypeStruct((DEPTH, B, 1, HEADS), F32),
        jax.ShapeDtypeStruct((DEPTH, B, HEADS, DK, DV), F32),
    ]
    z_scratch = [pltpu.VMEM((L, Z_BIG), BF16), pltpu.VMEM((L, Z_SMALL), F32)]
    outs = pl.pallas_call(
        functools.partial(_front_kernel, L=L, seg=seg, pipelined=pipelined, aliased=states_out is not None),
        grid=grid,
        in_specs=in_specs,
        out_specs=[
            pl.BlockSpec((rows, HEADS * DV), row_map),
            pl.BlockSpec((rows, HEADS * DV), row_map),
            pl.BlockSpec((rows, Z_GATES), row_map),
            *state_specs(layer),
        ],
        out_shape=out_shape,
        input_output_aliases=aliases,
        scratch_shapes=z_scratch * 2 if pipelined else [],
        compiler_params=pltpu.CompilerParams(
            dimension_semantics=("arbitrary", "arbitrary"),
            vmem_limit_bytes=VMEM_LIMIT_BYTES,
        ),
        name="front",
    )(*args)
    return outs[0], outs[1], outs[2], tuple(outs[3:])


def _channel_kernel(*refs, seq_tiles, has_state, final):
    (h_ref, hm_ref, hg_ref, ga_ref, gb_ref, wpa_ref, wpb_ref, wo_ref, gffn_ref,
     wup_ref, cw_ref, cb_ref, wd_ref, gple_ref, wpg_ref, wpp_ref, p_ref) = refs[:17]
    pos = 17
    pst_ref = gfin_ref = carry_scr = None
    if has_state:
        pst_ref = refs[pos]
        pos += 1
    if final:
        gfin_ref = refs[pos]
        pos += 1
    out_ref, conv_ref = refs[pos:pos + 2]
    if not has_state:
        carry_scr = refs[pos + 2]

    i = pl.program_id(0)
    tm = h_ref.shape[0]

    pa = _dot(hm_ref[...], wpa_ref[...])
    pb = _dot(hg_ref[...], wpb_ref[...])
    merged = _sigmoid(ga_ref[...].astype(F32)) * pa + _sigmoid(gb_ref[...].astype(F32)) * pb
    h1 = h_ref[...] + _dot(merged.astype(BF16), wo_ref[...])
    c = _rmsnorm(h1, gffn_ref[...]).astype(BF16)

    up = [_dot(c, wup_ref[:, n * UP_CHUNK:(n + 1) * UP_CHUNK]) for n in range(2 * D_FF // UP_CHUNK)]

    def up_cols(start):
        n, off = divmod(start, UP_CHUNK)
        assert off + FF_CHUNK <= UP_CHUNK
        return up[n][:, off:off + FF_CHUNK]

    rowi = lax.broadcasted_iota(jnp.int32, (tm, FF_CHUNK), 0)
    acc = None
    for n in range(D_FF // FF_CHUNK):
        ff = slice(n * FF_CHUNK, (n + 1) * FF_CHUNK)
        ug = up_cols(n * FF_CHUNK)
        uv = up_cols(D_FF + n * FF_CHUNK)
        r1 = pltpu.roll(ug, 1, 0)
        r2 = pltpu.roll(ug, 2, 0)
        if has_state:
            pst = pst_ref[:, ff]
            t = rowi & (SUBLANE - 1)
            u1 = jnp.where(t == 0, pltpu.roll(pst, tm - 1, 0), r1)
            u2 = jnp.where(t < 2, pst, r2)
            conv_ref[:, ff] = ug
        else:
            prev = jnp.where(i % seq_tiles == 0, 0.0, carry_scr[:, ff])
            p6 = jnp.broadcast_to(prev[6:7, :], ug.shape)
            p7 = jnp.broadcast_to(prev[7:8, :], ug.shape)
            u1 = jnp.where(rowi == 0, p7, r1)
            u2 = jnp.where(rowi == 0, p6, jnp.where(rowi == 1, p7, r2))
            tail = ug[tm - SUBLANE:, :]
            carry_scr[:, ff] = tail
            conv_ref[:, ff] = tail
        conv = cb_ref[:, ff] + cw_ref[0:1, ff] * u2 + cw_ref[1:2, ff] * u1 + cw_ref[2:3, ff] * ug
        act = conv * _sigmoid(conv) * uv
        part = _dot(act.astype(BF16), wd_ref[ff, :])
        acc = part if acc is None else acc + part

    h2 = h1 + acc
    e = _rmsnorm(h2, gple_ref[...]).astype(BF16)
    h3 = h2 + _sigmoid(_dot(e, wpg_ref[...])) * _dot(p_ref[...].astype(BF16), wpp_ref[...])
    if final:
        h3 = _rmsnorm(h3, gfin_ref[...])
    out_ref[...] = h3


def _channel(h, hm, hg, zg, p, pst, layer, W, *, seq_rows, final):
    T = h.shape[0]
    tm = CHAN_ROWS
    has_state = pst is not None
    seq_tiles = max(seq_rows // tm, 1)
    rows = lambda i: (i, 0)
    par = lambda i: (layer, 0, 0)
    in_specs = [
        pl.BlockSpec((tm, D_MODEL), rows),
        pl.BlockSpec((tm, D_MODEL), rows),
        pl.BlockSpec((tm, D_MODEL), rows),
        pl.BlockSpec((tm, D_MODEL), lambda i: (i, (GA - Z_MIX) // D_MODEL)),
        pl.BlockSpec((tm, D_MODEL), lambda i: (i, (GB - Z_MIX) // D_MODEL)),
        _resident((None, D_MODEL, D_MODEL), par),
        _resident((None, D_MODEL, D_MODEL), par),
        _resident((None, D_MODEL, D_MODEL), par),
        _resident((None, 1, D_MODEL), par),
        _resident((None, D_MODEL, 2 * D_FF), par),
        _resident((None, CONV_W, D_FF), par),
        _resident((None, 1, D_FF), par),
        _resident((None, D_FF, D_MODEL), par),
        _resident((None, 1, D_MODEL), par),
        _resident((None, D_MODEL, D_MODEL), par),
        _resident((None, PLE_DIM, D_MODEL), par),
        pl.BlockSpec((None, tm, PLE_DIM), lambda i: (layer, i, 0)),
    ]
    args = [h, hm, hg, zg, zg, W["w_pa"], W["w_pb"], W["w_o"], W["norm_ffn_g"],
            W["w_up"], W["conv_w"], W["conv_b"], W["w_down"], W["norm_ple_g"],
            W["w_ple_gate"], W["w_ple_proj"], p]
    scratch = []
    if has_state:
        in_specs.append(pl.BlockSpec((tm, D_FF), rows))
        args.append(pst)
        conv_spec = pl.BlockSpec((tm, D_FF), rows)
        conv_shape = jax.ShapeDtypeStruct((T, D_FF), F32)
    else:
        conv_spec = pl.BlockSpec((None, SUBLANE, D_FF), lambda i: (i, 0, 0))
        conv_shape = jax.ShapeDtypeStruct((T // tm, SUBLANE, D_FF), F32)
        scratch.append(pltpu.VMEM((SUBLANE, D_FF), F32))
    if final:
        in_specs.append(_resident((1, D_MODEL), lambda i: (0, 0)))
        args.append(W["final_norm_g"])
    return pl.pallas_call(
        functools.partial(_channel_kernel, seq_tiles=seq_tiles, has_state=has_state, final=final),
        grid=(T // tm,),
        in_specs=in_specs,
        out_specs=[pl.BlockSpec((tm, D_MODEL), rows), conv_spec],
        out_shape=[jax.ShapeDtypeStruct((T, D_MODEL), F32), conv_shape],
        scratch_shapes=scratch,
        compiler_params=pltpu.CompilerParams(
            dimension_semantics=("arbitrary",),
            vmem_limit_bytes=VMEM_LIMIT_BYTES,
        ),
        name="channel",
    )(*args)


def _prep_weights(norm_mix_g, w_in, b_mi, b_mf, g_mnorm, w_lr2, b_lr2, g_gnorm, w_pa, w_pb, w_o,
                  norm_ffn_g, w_up, conv_w, conv_b, w_down, norm_ple_g, w_ple_gate, w_ple_proj,
                  final_norm_g):
    gate_bias = jnp.concatenate([b_mi, b_mf, jnp.zeros((DEPTH, Z_SMALL - 2 * HEADS), F32)], axis=-1)
    w_lr2_pad = jnp.zeros((DEPTH, Z_SMALL, HEADS * DK), F32).at[:, LR_LANE:LR_LANE + G_RANK, :].set(w_lr2)
    row = lambda x: x.reshape(DEPTH, 1, x.shape[-1])
    w_in_t = jnp.swapaxes(w_in, 1, 2)
    return dict(
        norm_mix_g=row(norm_mix_g), w_big=_repack_w_in(w_in_t), w_small=_narrow_w_in(w_in_t),
        gate_bias=row(gate_bias), w_lr2=w_lr2_pad.astype(BF16), b_lr2=row(b_lr2),
        g_mnorm=row(g_mnorm), g_gnorm=row(g_gnorm),
        w_pa=w_pa.astype(BF16), w_pb=w_pb.astype(BF16), w_o=w_o.astype(BF16),
        norm_ffn_g=row(norm_ffn_g), w_up=w_up.astype(BF16), conv_w=conv_w, conv_b=row(conv_b),
        w_down=w_down.astype(BF16), norm_ple_g=row(norm_ple_g),
        w_ple_gate=w_ple_gate.astype(BF16), w_ple_proj=w_ple_proj.astype(BF16),
        final_norm_g=final_norm_g.reshape(1, D_MODEL),
    )


def _trunk(x, p, states, stacked, cv0, W, *, seq_rows):
    T = x.shape[0]
    B = T // seq_rows
    if seq_rows >= 2 * PROMPT_CHUNK:
        L, seg = PROMPT_CHUNK, PROMPT_CHUNK
    else:
        L, seg = SAMPLE_GROUP * seq_rows, seq_rows
    h = x
    new_states = None
    cvs = []
    for layer in range(DEPTH):
        hm, hg, zg, new_states = _front(h, states, layer if stacked else 0, new_states, layer, W,
                                        L=L, seg=seg, seq_rows=seq_rows)
        if cv0 is None:
            pst = None
        else:
            pst = jnp.pad(cv0[layer], ((0, 0), (0, seq_rows - (CONV_W - 1)), (0, 0))).reshape(T, D_FF)
        h, cv = _channel(h, hm, hg, zg, p, pst, layer, W, seq_rows=seq_rows, final=layer == DEPTH - 1)
        if cv0 is None:
            cv = cv.reshape(B, -1, SUBLANE, D_FF)[:, -1, SUBLANE - (CONV_W - 1):, :]
        else:
            cv = cv.reshape(B, seq_rows, D_FF)[:, seq_rows - (CONV_W - 1):, :]
        cvs.append(cv)
    c, n, m, s = new_states
    return h, c, n, m.reshape(DEPTH, B, HEADS), s, jnp.stack(cvs)


def kernel(x_prompt, x_sample, state_mlstm_C, state_mlstm_n, state_mlstm_m, state_gla_S, state_ffn_conv, p_prompt, p_sample, norm_mix_g, w_in, b_mi, b_mf, g_mnorm, w_lr2, b_lr2, g_gnorm, w_pa, w_pb, w_o, norm_ffn_g, w_up, conv_w, conv_b, w_down, norm_ple_g, w_ple_gate, w_ple_proj, final_norm_g):
    W = _prep_weights(norm_mix_g, w_in, b_mi, b_mf, g_mnorm, w_lr2, b_lr2, g_gnorm, w_pa, w_pb, w_o,
                      norm_ffn_g, w_up, conv_w, conv_b, w_down, norm_ple_g, w_ple_gate, w_ple_proj,
                      final_norm_g)
    bp, sp, _ = x_prompt.shape
    bs, ss, _ = x_sample.shape
    zero_states = (jnp.zeros((1, bp, HEADS, DK, DV), F32), jnp.zeros((1, bp, HEADS, DK), F32),
                   jnp.zeros((1, bp, 1, HEADS), F32), jnp.zeros((1, bp, HEADS, DK, DV), F32))
    yp, c_p, n_p, m_p, s_p, cv_p = _trunk(
        x_prompt.reshape(bp * sp, D_MODEL), p_prompt.reshape(DEPTH, bp * sp, PLE_DIM),
        zero_states, False, None, W, seq_rows=sp)
    sample_states = (state_mlstm_C, state_mlstm_n, state_mlstm_m.reshape(DEPTH, bs, 1, HEADS), state_gla_S)
    ys, c_s, n_s, m_s, s_s, cv_s = _trunk(
        x_sample.reshape(bs * ss, D_MODEL), p_sample.reshape(DEPTH, bs * ss, PLE_DIM),
        sample_states, True, state_ffn_conv, W, seq_rows=ss)
    return (yp.reshape(bp, sp, D_MODEL), ys.reshape(bs, ss, D_MODEL),
            c_p, n_p, m_p, s_p, cv_p, c_s, n_s, m_s, s_s, cv_s)
```

```python
import functools

import jax
import jax.numpy as jnp
from jax import lax
from jax.experimental import pallas as pl
from jax.experimental.pallas import tpu as pltpu

F32 = jnp.float32
BF16 = jnp.bfloat16
HIGHEST = lax.Precision.HIGHEST

D_MODEL = 1024
DEPTH = 4
HEADS = 4
DK = 128
DV = 256
G_RANK = 16
G_TAU = 16.0
D_FF = 2688
CONV_W = 3
PLE_DIM = 256
EPS = 1e-6
SCALE_K = DK ** -0.5

MQ, MK, MV, MO = 0, 512, 1024, 2048
GQ, GK, GV, GG = 3072, 3584, 4096, 5120
GA, GB = 6144, 7168
Z_BIG = 8192
Z_MIX = 6144
Z_GATES = Z_BIG - Z_MIX
Z_SMALL = 128
LR_LANE = 8
SRC_MI, SRC_GLR = 2048, 5128
SHIFT_AFTER_MI = 2 * HEADS
SHIFT_AFTER_GLR = 2 * HEADS + G_RANK

LANE = 128
SUBLANE = 8
VMEM_LIMIT_BYTES = 58 * 1024 * 1024

PROJ_ROWS = 512
PROJ_COLS = 256
CHAN_ROWS = 256
UP_CHUNK = 1792
FF_CHUNK = 896
REPACK_COLS = 512
PROMPT_CHUNK = 128
SAMPLE_GROUP = 4
GLA_SAFE_LOG_DECAY = -60.0


def _dot(a, b):
    return jnp.dot(a, b, preferred_element_type=F32)


def _dot_nt(a, b):
    return lax.dot_general(a, b, (((1,), (1,)), ((), ())), preferred_element_type=F32)


def _dot_tn(a, b):
    return lax.dot_general(a, b, (((0,), (0,)), ((), ())), preferred_element_type=F32)


def _log_sigmoid(x):
    return jnp.minimum(x, 0.0) - jnp.log(1.0 + jnp.exp(-jnp.abs(x)))


def _masked_cumsum(mask_bf16, x, pieces):
    acc = None
    rest = x
    for n in range(pieces):
        piece = rest.astype(BF16)
        part = _dot(mask_bf16, piece)
        acc = part if acc is None else acc + part
        if n + 1 < pieces:
            rest = rest - piece.astype(F32)
    return acc


def _sigmoid(x):
    return 0.5 * jnp.tanh(0.5 * x) + 0.5


def _rmsnorm(x, g):
    return x * lax.rsqrt(jnp.mean(x * x, axis=-1, keepdims=True) + EPS) * g


def _resident(block_shape, index_map):
    return pl.BlockSpec(block_shape, index_map, pipeline_mode=pl.Buffered(1))


REPACK_TAIL = 32


def _repack_kernel(a_ref, b_ref, o_ref):
    t = pl.program_id(1)
    first_after_mi = MO // REPACK_COLS
    first_after_glr = GG // REPACK_COLS

    def emit(shift):
        if shift == 0:
            y = a_ref[...]
        else:
            x = jnp.concatenate([a_ref[...], b_ref[...]], axis=0)
            y = x[shift:shift + REPACK_COLS, :]
        o_ref[...] = y.T.astype(BF16)

    @pl.when(t < first_after_mi)
    def _():
        emit(0)

    @pl.when((t >= first_after_mi) & (t < first_after_glr))
    def _():
        emit(SHIFT_AFTER_MI)

    @pl.when(t >= first_after_glr)
    def _():
        emit(SHIFT_AFTER_GLR)


def _repack_w_in(w_in_t):
    tails_per_tile = REPACK_COLS // REPACK_TAIL
    return pl.pallas_call(
        _repack_kernel,
        grid=(DEPTH, Z_BIG // REPACK_COLS),
        in_specs=[
            pl.BlockSpec((None, REPACK_COLS, D_MODEL), lambda l, t: (l, t, 0)),
            pl.BlockSpec((None, REPACK_TAIL, D_MODEL), lambda l, t: (l, tails_per_tile * (t + 1), 0)),
        ],
        out_specs=pl.BlockSpec((None, D_MODEL, REPACK_COLS), lambda l, t: (l, 0, t)),
        out_shape=jax.ShapeDtypeStruct((DEPTH, D_MODEL, Z_BIG), BF16),
        compiler_params=pltpu.CompilerParams(
            dimension_semantics=("arbitrary", "arbitrary"),
            vmem_limit_bytes=VMEM_LIMIT_BYTES,
        ),
        name="repack_w_in",
    )(w_in_t, w_in_t)


def _narrow_kernel(mi_ref, glr0_ref, glr1_ref, o_ref):
    pad = jnp.zeros((Z_SMALL - SHIFT_AFTER_GLR, D_MODEL), F32)
    x = jnp.concatenate([mi_ref[...], glr0_ref[...], glr1_ref[...], pad], axis=0)
    o_ref[...] = x.T.astype(BF16)


def _narrow_w_in(w_in_t):
    assert SRC_MI % SUBLANE == 0 and SRC_GLR % SUBLANE == 0 and G_RANK == 2 * SUBLANE
    rows8 = lambda r: pl.BlockSpec((None, SUBLANE, D_MODEL), lambda l: (l, r // SUBLANE, 0))
    return pl.pallas_call(
        _narrow_kernel,
        grid=(DEPTH,),
        in_specs=[rows8(SRC_MI), rows8(SRC_GLR), rows8(SRC_GLR + SUBLANE)],
        out_specs=pl.BlockSpec((None, D_MODEL, Z_SMALL), lambda l: (l, 0, 0)),
        out_shape=jax.ShapeDtypeStruct((DEPTH, D_MODEL, Z_SMALL), BF16),
        compiler_params=pltpu.CompilerParams(dimension_semantics=("arbitrary",)),
        name="narrow_w_in",
    )(w_in_t, w_in_t, w_in_t)


def _in_proj_kernel(h_ref, g_ref, wbig_ref, wsm_ref, zbig_ref, zsm_ref):
    a = _rmsnorm(h_ref[...], g_ref[...]).astype(BF16)
    zsm_ref[...] = _dot(a, wsm_ref[...])
    for n in range(Z_BIG // PROJ_COLS):
        cols = slice(n * PROJ_COLS, (n + 1) * PROJ_COLS)
        zbig_ref[:, cols] = _dot(a, wbig_ref[:, cols])


def _in_proj(h, layer, W):
    T = h.shape[0]
    par = lambda i: (layer, 0, 0)
    return pl.pallas_call(
        _in_proj_kernel,
        grid=(T // PROJ_ROWS,),
        in_specs=[
            pl.BlockSpec((PROJ_ROWS, D_MODEL), lambda i: (i, 0)),
            _resident((None, 1, D_MODEL), par),
            _resident((None, D_MODEL, Z_BIG), par),
            _resident((None, D_MODEL, Z_SMALL), par),
        ],
        out_specs=[
            pl.BlockSpec((PROJ_ROWS, Z_BIG), lambda i: (i, 0)),
            pl.BlockSpec((PROJ_ROWS, Z_SMALL), lambda i: (i, 0)),
        ],
        out_shape=[
            jax.ShapeDtypeStruct((T, Z_BIG), F32),
            jax.ShapeDtypeStruct((T, Z_SMALL), F32),
        ],
        compiler_params=pltpu.CompilerParams(
            dimension_semantics=("arbitrary",),
            vmem_limit_bytes=VMEM_LIMIT_BYTES,
        ),
        name="in_proj",
    )(h, W["norm_mix_g"], W["w_big"], W["w_small"])


N_PROJ_STAGES = 1 + Z_BIG // PROJ_COLS
MIX_STAGES = 1 + 2 * HEADS


def _project_stages(h_ref, r0, rows, g_ref, wbig_ref, wsm_ref, z_ref, zs_ref):
    held = {}

    def head():
        a = _rmsnorm(h_ref[r0:r0 + rows, :], g_ref[...]).astype(BF16)
        held["a"] = a
        zs_ref[...] = _dot(a, wsm_ref[...])

    def body(n):
        cols = slice(n * PROJ_COLS, (n + 1) * PROJ_COLS)
        z_ref[:, cols] = _dot(held["a"], wbig_ref[:, cols])

    return [head] + [functools.partial(body, n) for n in range(Z_BIG // PROJ_COLS)]


def _anchor(x, half):
    rows, width = x.shape
    if half >= SUBLANE:
        parts = []
        for blk in range(rows // (2 * half)):
            r = blk * 2 * half + half - 1
            parts.append(jnp.broadcast_to(x[r:r + 1, :], (2 * half, width)))
        return parts[0] if len(parts) == 1 else jnp.concatenate(parts, axis=0)
    x3 = x.reshape(rows // SUBLANE, SUBLANE, width)
    sub = lax.broadcasted_iota(jnp.int32, x3.shape, 1)

    def bc(i):
        return jnp.broadcast_to(x3[:, i:i + 1, :], x3.shape)

    if half == 4:
        out = bc(3)
    elif half == 2:
        out = jnp.where(sub < 4, bc(1), bc(5))
    else:
        out = jnp.where(sub < 2, bc(0), jnp.where(sub < 4, bc(2), jnp.where(sub < 6, bc(4), bc(6))))
    return out.reshape(rows, width)


def _causal_mask(L, seg):
    row = lax.broadcasted_iota(jnp.int32, (L, L), 0)
    col = lax.broadcasted_iota(jnp.int32, (L, L), 1)
    return (col <= row) & ((row ^ col) < seg)


def _mix_chunk(zb_ref, zs_ref, r0, state_in, params, outs, *, L, seg, make_fillers=None):
    w2_ref, blr_ref = params[1:3]
    tril_b = _causal_mask(L, seg).astype(F32).astype(BF16)
    la = _log_sigmoid(_dot(zs_ref[...].astype(BF16), w2_ref[...]) + blr_ref[...]) * (1.0 / G_TAU)
    bc = _masked_cumsum(tril_b, la, 2)

    def run(single_anchor):
        fillers = make_fillers() if make_fillers else ()
        _mix_rest(zb_ref, zs_ref, bc, r0, state_in, params, outs, L=L, seg=seg,
                  single_anchor=single_anchor, fillers=fillers)

    lax.cond(jnp.min(bc) >= GLA_SAFE_LOG_DECAY, lambda: run(True), lambda: run(False))


def _mix_rest(zb_ref, zs_ref, bc, r0, state_in, params, outs, *, L, seg, single_anchor, fillers):
    cin, nin, min_, sin = state_in
    bias_ref, w2_ref, blr_ref, gm_ref, gg_ref = params
    hm_ref, hg_ref, zg_ref, c_ref, n_ref, m_ref, s_ref = outs
    orow = slice(r0, r0 + L)
    nseg = L // seg
    seg_shift = seg.bit_length() - 1

    row = lax.broadcasted_iota(jnp.int32, (L, L), 0)
    col = lax.broadcasted_iota(jnp.int32, (L, L), 1)
    rxc = row ^ col
    causal = (col <= row) & (rxc < seg)
    tril_b = causal.astype(F32).astype(BF16)
    lane = lax.broadcasted_iota(jnp.int32, (L, LANE), 1)
    rowl = lax.broadcasted_iota(jnp.int32, (L, LANE), 0)
    eye_k = (lax.broadcasted_iota(jnp.int32, (DK, DK), 0)
             == lax.broadcasted_iota(jnp.int32, (DK, DK), 1))
    lane4 = lax.broadcasted_iota(jnp.int32, (1, HEADS), 1)

    def seg_last(x):
        parts = [jnp.broadcast_to(x[(g + 1) * seg - 1:(g + 1) * seg, :], (seg, x.shape[1]))
                 for g in range(nseg)]
        return parts[0] if nseg == 1 else jnp.concatenate(parts, axis=0)

    def seg_rows(vals):
        parts = [jnp.broadcast_to(v, (seg, v.shape[1])) for v in vals]
        return parts[0] if nseg == 1 else jnp.concatenate(parts, axis=0)

    def seg_mask(x, g):
        return x if nseg == 1 else jnp.where((rowl >> seg_shift) == g, x, 0.0)

    def per_seg_dot(x_f32, state_ref, h):
        parts = [_dot(x_f32[g * seg:(g + 1) * seg, :].astype(BF16), state_ref[g, h].astype(BF16))
                 for g in range(nseg)]
        return parts[0] if nseg == 1 else jnp.concatenate(parts, axis=0)

    fillers = list(fillers)

    per_slot = -(-len(fillers) // MIX_STAGES)

    def next_filler():
        for _ in range(min(per_slot, len(fillers))):
            fillers.pop(0)()

    next_filler()
    zg_ref[orow, :] = zb_ref[:, Z_MIX:]

    zs = zs_ref[...]
    gates = zs + bias_ref[...]
    lf = _log_sigmoid(gates)
    cum = _masked_cumsum(tril_b, lf, 3)
    cols = jnp.where(lane < HEADS, gates, cum)
    sel = (lax.broadcasted_iota(jnp.int32, (SUBLANE, LANE), 0)
           == lax.broadcasted_iota(jnp.int32, (SUBLANE, LANE), 1)).astype(F32)
    rows_ = lax.dot_general(sel, cols, (((1,), (1,)), ((), ())),
                            precision=HIGHEST, preferred_element_type=F32)

    def lhs_with_state(a_intra, x_inter):
        return jnp.concatenate([a_intra.astype(BF16), x_inter.astype(BF16)], axis=1)

    def rhs_with_state(v, state_ref, h):
        return jnp.concatenate([v, state_ref[0, h].astype(BF16)], axis=0)

    m_old = [min_[g] for g in range(nseg)]
    m_new = [jnp.zeros((1, HEADS), F32) for _ in range(nseg)]

    heads = range(HEADS)

    qf = [zb_ref[:, MQ + DK * h:MQ + DK * (h + 1)] for h in heads]
    kf = [zb_ref[:, MK + DK * h:MK + DK * (h + 1)] for h in heads]
    vm = [zb_ref[:, MV + DV * h:MV + DV * (h + 1)].astype(BF16) for h in heads]
    li_c = [cols[:, h:h + 1] for h in heads]
    b_c = [cols[:, HEADS + h:HEADS + h + 1] for h in heads]
    mt, w_inter, amat_m = [], [], []
    for h in heads:
        li_r = rows_[h:h + 1, :]
        b_r = rows_[HEADS + h:HEADS + h + 1, :]
        m_col = seg_rows([m_old[g][:, h:h + 1] for g in range(nseg)])
        dmat = jnp.where(causal, b_c[h] + (li_r - b_r), -jnp.inf)
        inter = b_c[h] + m_col
        mt.append(jnp.maximum(inter, jnp.max(dmat, axis=1, keepdims=True)))
        w_inter.append(jnp.exp(inter - mt[h]))
        scores = _dot_nt(qf[h].astype(BF16), kf[h].astype(BF16))
        amat_m.append(jnp.exp(dmat - mt[h]) * (scores * SCALE_K))
    next_filler()

    qs = [zb_ref[:, GQ + DK * h:GQ + DK * (h + 1)] for h in heads]
    ks = [zb_ref[:, GK + DK * h:GK + DK * (h + 1)] for h in heads]
    vg = [zb_ref[:, GV + DV * h:GV + DV * (h + 1)].astype(BF16) for h in heads]
    b_h = [bc[:, DK * h:DK * (h + 1)] for h in heads]
    qe = [qs[h] * jnp.exp(b_h[h]) for h in heads]

    def intra_single_anchor():
        out = []
        for h in heads:
            ke = ks[h] * jnp.exp(-b_h[h])
            out.append(jnp.where(causal, _dot_nt(qe[h].astype(BF16), ke.astype(BF16)), 0.0))
        return out

    def intra_by_levels():
        out = [jnp.where(row == col, jnp.sum(qs[h] * ks[h], axis=1, keepdims=True), 0.0)
               for h in heads]
        half = 1
        while half < seg:
            wgt = jnp.exp(-jnp.abs(bc - _anchor(bc, half)))
            pair = (col < row) & (rxc >= half) & (rxc < 2 * half)
            for h in heads:
                w_h = wgt[:, DK * h:DK * (h + 1)]
                lvl = _dot_nt((qs[h] * w_h).astype(BF16), (ks[h] * w_h).astype(BF16))
                out[h] = jnp.where(pair, lvl, out[h])
            half *= 2
        return out

    amat_g = intra_single_anchor() if single_anchor else intra_by_levels()
    next_filler()

    hh = []
    for h in heads:
        n_rows = seg_rows([nin[g, h:h + 1, :] for g in range(nseg)])
        qn = jnp.sum(qf[h] * n_rows, axis=1, keepdims=True)
        if nseg == 1:
            num = _dot(lhs_with_state(amat_m[h], w_inter[h] * qf[h]), rhs_with_state(vm[h], cin, h))
        else:
            num = _dot(amat_m[h].astype(BF16), vm[h]) + w_inter[h] * per_seg_dot(qf[h], cin, h)
        den = jnp.sum(amat_m[h], axis=1, keepdims=True) + w_inter[h] * qn
        den = jnp.maximum(jnp.abs(den), jnp.exp(-mt[h]))
        hh.append(num * (1.0 / den))
    next_filler()

    og_ = []
    for h in heads:
        if nseg == 1:
            o = _dot(lhs_with_state(amat_g[h] * SCALE_K, qe[h] * SCALE_K), rhs_with_state(vg[h], sin, h))
        else:
            o = _dot((amat_g[h] * SCALE_K).astype(BF16), vg[h]) + SCALE_K * per_seg_dot(qe[h], sin, h)
        og_.append(o)
    next_filler()

    for h in heads:
        mu = jnp.mean(hh[h], axis=-1, keepdims=True)
        hc = hh[h] - mu
        hn = hc * lax.rsqrt(jnp.mean(hc * hc, axis=-1, keepdims=True) + EPS)
        og = zb_ref[:, MO + DV * h:MO + DV * (h + 1)]
        hm_ref[orow, DV * h:DV * (h + 1)] = (hn * gm_ref[:, DV * h:DV * (h + 1)] * _sigmoid(og)).astype(BF16)
    next_filler()

    for h in heads:
        o = og_[h]
        on = o * lax.rsqrt(jnp.mean(o * o, axis=-1, keepdims=True) + EPS)
        gg = zb_ref[:, GG + DV * h:GG + DV * (h + 1)]
        hg_ref[orow, DV * h:DV * (h + 1)] = (on * gg_ref[:, DV * h:DV * (h + 1)] * (gg * _sigmoid(gg))).astype(BF16)
    next_filler()

    for h in heads:
        ws = jnp.exp(seg_last(b_c[h]) - b_c[h] + li_c[h] - seg_last(mt[h])) * SCALE_K
        kw = kf[h] * ws
        for g in range(nseg):
            last = (g + 1) * seg - 1
            m_last = mt[h][last:last + 1, :]
            cw = jnp.exp(b_c[h][last:last + 1, :] + m_old[g][:, h:h + 1] - m_last)
            c_new = cw * cin[g, h] + _dot_tn(seg_mask(kw, g).astype(BF16), vm[h])
            n_new = cw * nin[g, h:h + 1, :] + jnp.sum(kw[g * seg:(g + 1) * seg, :], axis=0, keepdims=True)
            c_ref[g, h] = c_new
            n_ref[g, h:h + 1, :] = n_new
            m_new[g] = jnp.where(lane4 == h, m_last, m_new[g])
    for g in range(nseg):
        m_ref[g] = m_new[g]
    next_filler()

    bl_rows = seg_last(bc)
    for h in heads:
        kt = ks[h] * jnp.exp(bl_rows[:, DK * h:DK * (h + 1)] - b_h[h])
        for g in range(nseg):
            last = (g + 1) * seg - 1
            dec_row = jnp.exp(b_h[h][last:last + 1, :])
            dec_col = jnp.sum(jnp.where(eye_k, jnp.broadcast_to(dec_row, (DK, DK)), 0.0),
                              axis=1, keepdims=True)
            s_ref[g, h] = dec_col * sin[g, h] + _dot_tn(seg_mask(kt, g).astype(BF16), vg[h])
    next_filler()
    assert not fillers


def _front_kernel(*refs, L, seg, pipelined, aliased):
    lead = 5 if pipelined else 2
    pos = lead
    c0_ref, n0_ref, m0_ref, s0_ref = refs[pos:pos + 4]
    pos += 4
    params = refs[pos:pos + 5]
    pos += 5 + (4 if aliased else 0)
    outs = refs[pos:pos + 7]
    pos += 7
    scratch = refs[pos:]
    c_ref, n_ref, m_ref, s_ref = outs[3:]

    if not pipelined:
        z_ref, zs_ref = refs[:lead]
        _mix_chunk(z_ref, zs_ref, 0, (c0_ref, n0_ref, m0_ref, s0_ref), params, outs, L=L, seg=seg)
        return

    h_cur, h_nxt, g_ref, wbig_ref, wsm_ref = refs[:lead]
    stages = functools.partial(_project_stages, g_ref=g_ref, wbig_ref=wbig_ref, wsm_ref=wsm_ref)
    z_a, zs_a, z_b, zs_b = scratch

    @pl.when(pl.program_id(1) == 0)
    def _():
        c_ref[...] = c0_ref[...]
        n_ref[...] = n0_ref[...]
        m_ref[...] = m0_ref[...]
        s_ref[...] = s0_ref[...]
        for stage in stages(h_cur, 0, L, z_ref=z_a, zs_ref=zs_a):
            stage()

    carried = (c_ref, n_ref, m_ref, s_ref)
    _mix_chunk(z_a, zs_a, 0, carried, params, outs, L=L, seg=seg,
               make_fillers=lambda: stages(h_cur, L, L, z_ref=z_b, zs_ref=zs_b))
    _mix_chunk(z_b, zs_b, L, carried, params, outs, L=L, seg=seg,
               make_fillers=lambda: stages(h_nxt, 0, L, z_ref=z_a, zs_ref=zs_a))


def _front(h, states_in, in_layer, states_out, layer, W, *, L, seg, seq_rows):
    T = h.shape[0]
    nseg = L // seg
    B = states_in[0].shape[1]
    pipelined = seq_rows > L
    rows = 2 * L if pipelined else L
    steps = seq_rows // rows if pipelined else 1
    grid = (B // nseg, steps)
    row_map = lambda b, k: (b * steps + k, 0)
    nxt_map = lambda b, k: (b * steps + jnp.minimum(k + 1, steps - 1), 0)
    par = lambda b, k: (layer, 0, 0)

    def state_specs(at):
        return [
            pl.BlockSpec((None, nseg, HEADS, DK, DV), lambda b, k: (at, b, 0, 0, 0)),
            pl.BlockSpec((None, nseg, HEADS, DK), lambda b, k: (at, b, 0, 0)),
            pl.BlockSpec((None, nseg, 1, HEADS), lambda b, k: (at, b, 0, 0)),
            pl.BlockSpec((None, nseg, HEADS, DK, DV), lambda b, k: (at, b, 0, 0, 0)),
        ]

    if pipelined:
        in_specs = [
            pl.BlockSpec((rows, D_MODEL), row_map),
            pl.BlockSpec((rows, D_MODEL), nxt_map),
            _resident((None, 1, D_MODEL), par),
            _resident((None, D_MODEL, Z_BIG), par),
            _resident((None, D_MODEL, Z_SMALL), par),
        ]
        args = [h, h, W["norm_mix_g"], W["w_big"], W["w_small"]]
    else:
        in_specs = [pl.BlockSpec((rows, Z_BIG), row_map), pl.BlockSpec((rows, Z_SMALL), row_map)]
        args = list(_in_proj(h, layer, W))
    in_specs += [
        *state_specs(in_layer),
        _resident((None, 1, Z_SMALL), par),
        _resident((None, Z_SMALL, HEADS * DK), par),
        _resident((None, 1, HEADS * DK), par),
        _resident((None, 1, HEADS * DV), par),
        _resident((None, 1, HEADS * DV), par),
    ]
    args += [*states_in, W["gate_bias"], W["w_lr2"], W["b_lr2"], W["g_mnorm"], W["g_gnorm"]]
    aliases = {}
    if states_out is not None:
        for n, s in enumerate(states_out):
            aliases[len(args)] = 3 + n
            in_specs.append(pl.BlockSpec(memory_space=pl.ANY))
            args.append(s)
    out_shape = [
        jax.ShapeDtypeStruct((T, HEADS * DV), BF16),
        jax.ShapeDtypeStruct((T, HEADS * DV), BF16),
        jax.ShapeDtypeStruct((T, Z_GATES), F32),
        jax.ShapeDtypeStruct((DEPTH, B, HEADS, DK, DV), F32),
        jax.ShapeDtypeStruct((DEPTH, B, HEADS, DK), F32),
        jax.ShapeDtypeStruct((DEPTH, B, 1, HEADS), F32),
        jax.ShapeDtypeStruct((DEPTH, B, HEADS, DK, DV), F32),
    ]
    z_scratch = [pltpu.VMEM((L, Z_BIG), F32), pltpu.VMEM((L, Z_SMALL), F32)]
    outs = pl.pallas_call(
        functools.partial(_front_kernel, L=L, seg=seg, pipelined=pipelined, aliased=states_out is not None),
        grid=grid,
        in_specs=in_specs,
        out_specs=[
            pl.BlockSpec((rows, HEADS * DV), row_map),
            pl.BlockSpec((rows, HEADS * DV), row_map),
            pl.BlockSpec((rows, Z_GATES), row_map),
            *state_specs(layer),
        ],
        out_shape=out_shape,
        input_output_aliases=aliases,
        scratch_shapes=z_scratch * 2 if pipelined else [],
        compiler_params=pltpu.CompilerParams(
            dimension_semantics=("arbitrary", "arbitrary"),
            vmem_limit_bytes=VMEM_LIMIT_BYTES,
        ),
        name="front",
    )(*args)
    return outs[0], outs[1], outs[2], tuple(outs[3:])


def _channel_kernel(*refs, seq_tiles, has_state, final):
    (h_ref, hm_ref, hg_ref, ga_ref, gb_ref, wpa_ref, wpb_ref, wo_ref, gffn_ref,
     wup_ref, cw_ref, cb_ref, wd_ref, gple_ref, wpg_ref, wpp_ref, p_ref) = refs[:17]
    pos = 17
    pst_ref = gfin_ref = carry_scr = None
    if has_state:
        pst_ref = refs[pos]
        pos += 1
    if final:
        gfin_ref = refs[pos]
        pos += 1
    out_ref, conv_ref = refs[pos:pos + 2]
    if not has_state:
        carry_scr = refs[pos + 2]

    i = pl.program_id(0)
    tm = h_ref.shape[0]

    pa = _dot(hm_ref[...], wpa_ref[...])
    pb = _dot(hg_ref[...], wpb_ref[...])
    merged = _sigmoid(ga_ref[...]) * pa + _sigmoid(gb_ref[...]) * pb
    h1 = h_ref[...] + _dot(merged.astype(BF16), wo_ref[...])
    c = _rmsnorm(h1, gffn_ref[...]).astype(BF16)

    up = [_dot(c, wup_ref[:, n * UP_CHUNK:(n + 1) * UP_CHUNK]) for n in range(2 * D_FF // UP_CHUNK)]

    def up_cols(start):
        n, off = divmod(start, UP_CHUNK)
        assert off + FF_CHUNK <= UP_CHUNK
        return up[n][:, off:off + FF_CHUNK]

    rowi = lax.broadcasted_iota(jnp.int32, (tm, FF_CHUNK), 0)
    acc = None
    for n in range(D_FF // FF_CHUNK):
        ff = slice(n * FF_CHUNK, (n + 1) * FF_CHUNK)
        ug = up_cols(n * FF_CHUNK)
        uv = up_cols(D_FF + n * FF_CHUNK)
        r1 = pltpu.roll(ug, 1, 0)
        r2 = pltpu.roll(ug, 2, 0)
        if has_state:
            pst = pst_ref[:, ff]
            t = rowi & (SUBLANE - 1)
            u1 = jnp.where(t == 0, pltpu.roll(pst, tm - 1, 0), r1)
            u2 = jnp.where(t < 2, pst, r2)
            conv_ref[:, ff] = ug
        else:
            prev = jnp.where(i % seq_tiles == 0, 0.0, carry_scr[:, ff])
            p6 = jnp.broadcast_to(prev[6:7, :], ug.shape)
            p7 = jnp.broadcast_to(prev[7:8, :], ug.shape)
            u1 = jnp.where(rowi == 0, p7, r1)
            u2 = jnp.where(rowi == 0, p6, jnp.where(rowi == 1, p7, r2))
            tail = ug[tm - SUBLANE:, :]
            carry_scr[:, ff] = tail
            conv_ref[:, ff] = tail
        conv = cb_ref[:, ff] + cw_ref[0:1, ff] * u2 + cw_ref[1:2, ff] * u1 + cw_ref[2:3, ff] * ug
        act = conv * _sigmoid(conv) * uv
        part = _dot(act.astype(BF16), wd_ref[ff, :])
        acc = part if acc is None else acc + part

    h2 = h1 + acc
    e = _rmsnorm(h2, gple_ref[...]).astype(BF16)
    h3 = h2 + _sigmoid(_dot(e, wpg_ref[...])) * _dot(p_ref[...].astype(BF16), wpp_ref[...])
    if final:
        h3 = _rmsnorm(h3, gfin_ref[...])
    out_ref[...] = h3


def _channel(h, hm, hg, zg, p, pst, layer, W, *, seq_rows, final):
    T = h.shape[0]
    tm = CHAN_ROWS
    has_state = pst is not None
    seq_tiles = max(seq_rows // tm, 1)
    rows = lambda i: (i, 0)
    par = lambda i: (layer, 0, 0)
    in_specs = [
        pl.BlockSpec((tm, D_MODEL), rows),
        pl.BlockSpec((tm, D_MODEL), rows),
        pl.BlockSpec((tm, D_MODEL), rows),
        pl.BlockSpec((tm, D_MODEL), lambda i: (i, (GA - Z_MIX) // D_MODEL)),
        pl.BlockSpec((tm, D_MODEL), lambda i: (i, (GB - Z_MIX) // D_MODEL)),
        _resident((None, D_MODEL, D_MODEL), par),
        _resident((None, D_MODEL, D_MODEL), par),
        _resident((None, D_MODEL, D_MODEL), par),
        _resident((None, 1, D_MODEL), par),
        _resident((None, D_MODEL, 2 * D_FF), par),
        _resident((None, CONV_W, D_FF), par),
        _resident((None, 1, D_FF), par),
        _resident((None, D_FF, D_MODEL), par),
        _resident((None, 1, D_MODEL), par),
        _resident((None, D_MODEL, D_MODEL), par),
        _resident((None, PLE_DIM, D_MODEL), par),
        pl.BlockSpec((None, tm, PLE_DIM), lambda i: (layer, i, 0)),
    ]
    args = [h, hm, hg, zg, zg, W["w_pa"], W["w_pb"], W["w_o"], W["norm_ffn_g"],
            W["w_up"], W["conv_w"], W["conv_b"], W["w_down"], W["norm_ple_g"],
            W["w_ple_gate"], W["w_ple_proj"], p]
    scratch = []
    if has_state:
        in_specs.append(pl.BlockSpec((tm, D_FF), rows))
        args.append(pst)
        conv_spec = pl.BlockSpec((tm, D_FF), rows)
        conv_shape = jax.ShapeDtypeStruct((T, D_FF), F32)
    else:
        conv_spec = pl.BlockSpec((None, SUBLANE, D_FF), lambda i: (i, 0, 0))
        conv_shape = jax.ShapeDtypeStruct((T // tm, SUBLANE, D_FF), F32)
        scratch.append(pltpu.VMEM((SUBLANE, D_FF), F32))
    if final:
        in_specs.append(_resident((1, D_MODEL), lambda i: (0, 0)))
        args.append(W["final_norm_g"])
    return pl.pallas_call(
        functools.partial(_channel_kernel, seq_tiles=seq_tiles, has_state=has_state, final=final),
        grid=(T // tm,),
        in_specs=in_specs,
        out_specs=[pl.BlockSpec((tm, D_MODEL), rows), conv_spec],
        out_shape=[jax.ShapeDtypeStruct((T, D_MODEL), F32), conv_shape],
        scratch_shapes=scratch,
        compiler_params=pltpu.CompilerParams(
            dimension_semantics=("arbitrary",),
            vmem_limit_bytes=VMEM_LIMIT_BYTES,
        ),
        name="channel",
    )(*args)


def _prep_weights(norm_mix_g, w_in, b_mi, b_mf, g_mnorm, w_lr2, b_lr2, g_gnorm, w_pa, w_pb, w_o,
                  norm_ffn_g, w_up, conv_w, conv_b, w_down, norm_ple_g, w_ple_gate, w_ple_proj,
                  final_norm_g):
    gate_bias = jnp.concatenate([b_mi, b_mf, jnp.zeros((DEPTH, Z_SMALL - 2 * HEADS), F32)], axis=-1)
    w_lr2_pad = jnp.zeros((DEPTH, Z_SMALL, HEADS * DK), F32).at[:, LR_LANE:LR_LANE + G_RANK, :].set(w_lr2)
    row = lambda x: x.reshape(DEPTH, 1, x.shape[-1])
    w_in_t = jnp.swapaxes(w_in, 1, 2)
    return dict(
        norm_mix_g=row(norm_mix_g), w_big=_repack_w_in(w_in_t), w_small=_narrow_w_in(w_in_t),
        gate_bias=row(gate_bias), w_lr2=w_lr2_pad.astype(BF16), b_lr2=row(b_lr2),
        g_mnorm=row(g_mnorm), g_gnorm=row(g_gnorm),
        w_pa=w_pa.astype(BF16), w_pb=w_pb.astype(BF16), w_o=w_o.astype(BF16),
        norm_ffn_g=row(norm_ffn_g), w_up=w_up.astype(BF16), conv_w=conv_w, conv_b=row(conv_b),
        w_down=w_down.astype(BF16), norm_ple_g=row(norm_ple_g),
        w_ple_gate=w_ple_gate.astype(BF16), w_ple_proj=w_ple_proj.astype(BF16),
        final_norm_g=final_norm_g.reshape(1, D_MODEL),
    )


def _trunk(x, p, states, stacked, cv0, W, *, seq_rows):
    T = x.shape[0]
    B = T // seq_rows
    if seq_rows >= 2 * PROMPT_CHUNK:
        L, seg = PROMPT_CHUNK, PROMPT_CHUNK
    else:
        L, seg = SAMPLE_GROUP * seq_rows, seq_rows
    h = x
    new_states = None
    cvs = []
    for layer in range(DEPTH):
        hm, hg, zg, new_states = _front(h, states, layer if stacked else 0, new_states, layer, W,
                                        L=L, seg=seg, seq_rows=seq_rows)
        if cv0 is None:
            pst = None
        else:
            pst = jnp.pad(cv0[layer], ((0, 0), (0, seq_rows - (CONV_W - 1)), (0, 0))).reshape(T, D_FF)
        h, cv = _channel(h, hm, hg, zg, p, pst, layer, W, seq_rows=seq_rows, final=layer == DEPTH - 1)
        if cv0 is None:
            cv = cv.reshape(B, -1, SUBLANE, D_FF)[:, -1, SUBLANE - (CONV_W - 1):, :]
        else:
            cv = cv.reshape(B, seq_rows, D_FF)[:, seq_rows - (CONV_W - 1):, :]
        cvs.append(cv)
    c, n, m, s = new_states
    return h, c, n, m.reshape(DEPTH, B, HEADS), s, jnp.stack(cvs)


def kernel(x_prompt, x_sample, state_mlstm_C, state_mlstm_n, state_mlstm_m, state_gla_S, state_ffn_conv, p_prompt, p_sample, norm_mix_g, w_in, b_mi, b_mf, g_mnorm, w_lr2, b_lr2, g_gnorm, w_pa, w_pb, w_o, norm_ffn_g, w_up, conv_w, conv_b, w_down, norm_ple_g, w_ple_gate, w_ple_proj, final_norm_g):
    W = _prep_weights(norm_mix_g, w_in, b_mi, b_mf, g_mnorm, w_lr2, b_lr2, g_gnorm, w_pa, w_pb, w_o,
                      norm_ffn_g, w_up, conv_w, conv_b, w_down, norm_ple_g, w_ple_gate, w_ple_proj,
                      final_norm_g)
    bp, sp, _ = x_prompt.shape
    bs, ss, _ = x_sample.shape
    zero_states = (jnp.zeros((1, bp, HEADS, DK, DV), F32), jnp.zeros((1, bp, HEADS, DK), F32),
                   jnp.zeros((1, bp, 1, HEADS), F32), jnp.zeros((1, bp, HEADS, DK, DV), F32))
    yp, c_p, n_p, m_p, s_p, cv_p = _trunk(
        x_prompt.reshape(bp * sp, D_MODEL), p_prompt.reshape(DEPTH, bp * sp, PLE_DIM),
        zero_states, False, None, W, seq_rows=sp)
    sample_states = (state_mlstm_C, state_mlstm_n, state_mlstm_m.reshape(DEPTH, bs, 1, HEADS), state_gla_S)
    ys, c_s, n_s, m_s, s_s, cv_s = _trunk(
        x_sample.reshape(bs * ss, D_MODEL), p_sample.reshape(DEPTH, bs * ss, PLE_DIM),
        sample_states, True, state_ffn_conv, W, seq_rows=ss)
    return (yp.reshape(bp, sp, D_MODEL), ys.reshape(bs, ss, D_MODEL),
            c_p, n_p, m_p, s_p, cv_p, c_s, n_s, m_s, s_s, cv_s)
```

```python
import functools

import jax
import jax.numpy as jnp
from jax import lax
from jax.experimental import pallas as pl
from jax.experimental.pallas import tpu as pltpu

F32 = jnp.float32
BF16 = jnp.bfloat16
HIGHEST = lax.Precision.HIGHEST

D_MODEL = 1024
DEPTH = 4
HEADS = 4
DK = 128
DV = 256
G_RANK = 16
G_TAU = 16.0
D_FF = 2688
CONV_W = 3
PLE_DIM = 256
EPS = 1e-6
SCALE_K = DK ** -0.5

MQ, MK, MV, MO = 0, 512, 1024, 2048
GQ, GK, GV, GG = 3072, 3584, 4096, 5120
GA, GB = 6144, 7168
Z_BIG = 8192
Z_MIX = 6144
Z_GATES = Z_BIG - Z_MIX
Z_SMALL = 128
LR_LANE = 8
SRC_MI, SRC_GLR = 2048, 5128
SHIFT_AFTER_MI = 2 * HEADS
SHIFT_AFTER_GLR = 2 * HEADS + G_RANK

LANE = 128
SUBLANE = 8
VMEM_LIMIT_BYTES = 60 * 1024 * 1024

PROJ_ROWS = 512
PROJ_COLS = 256
CHAN_ROWS = 256
CHAN_SPLIT = 2
UP_CHUNK = 1792
FF_CHUNK = 896
REPACK_COLS = 512
PROMPT_CHUNK = 256
FUSE_PROJECTION = True
SAMPLE_GROUP = 4
GLA_SAFE_LOG_DECAY = -60.0


def _dot(a, b):
    return jnp.dot(a, b, preferred_element_type=F32)


def _dot_nt(a, b):
    return lax.dot_general(a, b, (((1,), (1,)), ((), ())), preferred_element_type=F32)


def _dot_tn(a, b):
    return lax.dot_general(a, b, (((0,), (0,)), ((), ())), preferred_element_type=F32)


def _log_sigmoid(x):
    return jnp.minimum(x, 0.0) - jnp.log(1.0 + jnp.exp(-jnp.abs(x)))


def _masked_cumsum(mask_bf16, x, pieces):
    acc = None
    rest = x
    for n in range(pieces):
        piece = rest.astype(BF16)
        part = _dot(mask_bf16, piece)
        acc = part if acc is None else acc + part
        if n + 1 < pieces:
            rest = rest - piece.astype(F32)
    return acc


def _sigmoid(x):
    return 0.5 * jnp.tanh(0.5 * x) + 0.5


def _rmsnorm(x, g):
    return x * lax.rsqrt(jnp.mean(x * x, axis=-1, keepdims=True) + EPS) * g


def _resident(block_shape, index_map):
    return pl.BlockSpec(block_shape, index_map, pipeline_mode=pl.Buffered(1))


REPACK_TAIL = 32


def _repack_kernel(a_ref, b_ref, o_ref):
    t = pl.program_id(1)
    first_after_mi = MO // REPACK_COLS
    first_after_glr = GG // REPACK_COLS

    def emit(shift):
        if shift == 0:
            y = a_ref[...]
        else:
            x = jnp.concatenate([a_ref[...], b_ref[...]], axis=0)
            y = x[shift:shift + REPACK_COLS, :]
        o_ref[...] = y.T.astype(BF16)

    @pl.when(t < first_after_mi)
    def _():
        emit(0)

    @pl.when((t >= first_after_mi) & (t < first_after_glr))
    def _():
        emit(SHIFT_AFTER_MI)

    @pl.when(t >= first_after_glr)
    def _():
        emit(SHIFT_AFTER_GLR)


def _repack_w_in(w_in_t):
    tails_per_tile = REPACK_COLS // REPACK_TAIL
    return pl.pallas_call(
        _repack_kernel,
        grid=(DEPTH, Z_BIG // REPACK_COLS),
        in_specs=[
            pl.BlockSpec((None, REPACK_COLS, D_MODEL), lambda l, t: (l, t, 0)),
            pl.BlockSpec((None, REPACK_TAIL, D_MODEL), lambda l, t: (l, tails_per_tile * (t + 1), 0)),
        ],
        out_specs=pl.BlockSpec((None, D_MODEL, REPACK_COLS), lambda l, t: (l, 0, t)),
        out_shape=jax.ShapeDtypeStruct((DEPTH, D_MODEL, Z_BIG), BF16),
        compiler_params=pltpu.CompilerParams(
            dimension_semantics=("arbitrary", "arbitrary"),
            vmem_limit_bytes=VMEM_LIMIT_BYTES,
        ),
        name="repack_w_in",
    )(w_in_t, w_in_t)


def _narrow_kernel(mi_ref, glr0_ref, glr1_ref, o_ref):
    pad = jnp.zeros((Z_SMALL - SHIFT_AFTER_GLR, D_MODEL), F32)
    x = jnp.concatenate([mi_ref[...], glr0_ref[...], glr1_ref[...], pad], axis=0)
    o_ref[...] = x.T.astype(BF16)


def _narrow_w_in(w_in_t):
    assert SRC_MI % SUBLANE == 0 and SRC_GLR % SUBLANE == 0 and G_RANK == 2 * SUBLANE
    rows8 = lambda r: pl.BlockSpec((None, SUBLANE, D_MODEL), lambda l: (l, r // SUBLANE, 0))
    return pl.pallas_call(
        _narrow_kernel,
        grid=(DEPTH,),
        in_specs=[rows8(SRC_MI), rows8(SRC_GLR), rows8(SRC_GLR + SUBLANE)],
        out_specs=pl.BlockSpec((None, D_MODEL, Z_SMALL), lambda l: (l, 0, 0)),
        out_shape=jax.ShapeDtypeStruct((DEPTH, D_MODEL, Z_SMALL), BF16),
        compiler_params=pltpu.CompilerParams(dimension_semantics=("arbitrary",)),
        name="narrow_w_in",
    )(w_in_t, w_in_t, w_in_t)


def _in_proj_kernel(h_ref, g_ref, wbig_ref, wsm_ref, zbig_ref, zsm_ref):
    a = _rmsnorm(h_ref[...], g_ref[...]).astype(BF16)
    zsm_ref[...] = _dot(a, wsm_ref[...])
    for n in range(Z_BIG // PROJ_COLS):
        cols = slice(n * PROJ_COLS, (n + 1) * PROJ_COLS)
        zbig_ref[:, cols] = _dot(a, wbig_ref[:, cols])


def _in_proj(h, layer, W):
    T = h.shape[0]
    par = lambda i: (layer, 0, 0)
    return pl.pallas_call(
        _in_proj_kernel,
        grid=(T // PROJ_ROWS,),
        in_specs=[
            pl.BlockSpec((PROJ_ROWS, D_MODEL), lambda i: (i, 0)),
            _resident((None, 1, D_MODEL), par),
            _resident((None, D_MODEL, Z_BIG), par),
            _resident((None, D_MODEL, Z_SMALL), par),
        ],
        out_specs=[
            pl.BlockSpec((PROJ_ROWS, Z_BIG), lambda i: (i, 0)),
            pl.BlockSpec((PROJ_ROWS, Z_SMALL), lambda i: (i, 0)),
        ],
        out_shape=[
            jax.ShapeDtypeStruct((T, Z_BIG), F32),
            jax.ShapeDtypeStruct((T, Z_SMALL), F32),
        ],
        compiler_params=pltpu.CompilerParams(
            dimension_semantics=("arbitrary",),
            vmem_limit_bytes=VMEM_LIMIT_BYTES,
        ),
        name="in_proj",
    )(h, W["norm_mix_g"], W["w_big"], W["w_small"])


N_PROJ_STAGES = 1 + Z_BIG // PROJ_COLS
MIX_STAGES = 1 + 8 * HEADS


def _project_stages(h_ref, r0, rows, g_ref, wbig_ref, wsm_ref, z_ref, zs_ref):
    held = {}

    def head():
        a = _rmsnorm(h_ref[r0:r0 + rows, :], g_ref[...]).astype(BF16)
        held["a"] = a
        zs_ref[...] = _dot(a, wsm_ref[...])

    def body(n):
        cols = slice(n * PROJ_COLS, (n + 1) * PROJ_COLS)
        z_ref[:, cols] = _dot(held["a"], wbig_ref[:, cols])

    return [head] + [functools.partial(body, n) for n in range(Z_BIG // PROJ_COLS)]


def _anchor(x, half):
    rows, width = x.shape
    if half >= SUBLANE:
        parts = []
        for blk in range(rows // (2 * half)):
            r = blk * 2 * half + half - 1
            parts.append(jnp.broadcast_to(x[r:r + 1, :], (2 * half, width)))
        return parts[0] if len(parts) == 1 else jnp.concatenate(parts, axis=0)
    x3 = x.reshape(rows // SUBLANE, SUBLANE, width)
    sub = lax.broadcasted_iota(jnp.int32, x3.shape, 1)

    def bc(i):
        return jnp.broadcast_to(x3[:, i:i + 1, :], x3.shape)

    if half == 4:
        out = bc(3)
    elif half == 2:
        out = jnp.where(sub < 4, bc(1), bc(5))
    else:
        out = jnp.where(sub < 2, bc(0), jnp.where(sub < 4, bc(2), jnp.where(sub < 6, bc(4), bc(6))))
    return out.reshape(rows, width)


def _causal_mask(L, seg):
    row = lax.broadcasted_iota(jnp.int32, (L, L), 0)
    col = lax.broadcasted_iota(jnp.int32, (L, L), 1)
    return (col <= row) & ((row ^ col) < seg)


def _mix_chunk(zb_ref, zs_ref, r0, state_in, params, outs, *, L, seg, make_fillers=None):
    w2_ref, blr_ref = params[1:3]
    tril_b = _causal_mask(L, seg).astype(F32).astype(BF16)
    la = _log_sigmoid(_dot(zs_ref[...].astype(BF16), w2_ref[...]) + blr_ref[...]) * (1.0 / G_TAU)
    bc = _masked_cumsum(tril_b, la, 2)

    def run(single_anchor):
        fillers = make_fillers() if make_fillers else ()
        _mix_rest(zb_ref, zs_ref, bc, r0, state_in, params, outs, L=L, seg=seg,
                  single_anchor=single_anchor, fillers=fillers)

    lax.cond(jnp.min(bc) >= GLA_SAFE_LOG_DECAY, lambda: run(True), lambda: run(False))


def _mix_rest(zb_ref, zs_ref, bc, r0, state_in, params, outs, *, L, seg, single_anchor, fillers):
    cin, nin, min_, sin = state_in
    bias_ref, w2_ref, blr_ref, gm_ref, gg_ref = params
    hm_ref, hg_ref, zg_ref, c_ref, n_ref, m_ref, s_ref = outs
    orow = slice(r0, r0 + L)
    nseg = L // seg
    seg_shift = seg.bit_length() - 1

    row = lax.broadcasted_iota(jnp.int32, (L, L), 0)
    col = lax.broadcasted_iota(jnp.int32, (L, L), 1)
    rxc = row ^ col
    causal = (col <= row) & (rxc < seg)
    tril_b = causal.astype(F32).astype(BF16)
    lane = lax.broadcasted_iota(jnp.int32, (L, LANE), 1)
    rowl = lax.broadcasted_iota(jnp.int32, (L, LANE), 0)
    eye_k = (lax.broadcasted_iota(jnp.int32, (DK, DK), 0)
             == lax.broadcasted_iota(jnp.int32, (DK, DK), 1))
    lane4 = lax.broadcasted_iota(jnp.int32, (1, HEADS), 1)

    def seg_last(x):
        parts = [jnp.broadcast_to(x[(g + 1) * seg - 1:(g + 1) * seg, :], (seg, x.shape[1]))
                 for g in range(nseg)]
        return parts[0] if nseg == 1 else jnp.concatenate(parts, axis=0)

    def seg_rows(vals):
        parts = [jnp.broadcast_to(v, (seg, v.shape[1])) for v in vals]
        return parts[0] if nseg == 1 else jnp.concatenate(parts, axis=0)

    def seg_mask(x, g):
        return x if nseg == 1 else jnp.where((rowl >> seg_shift) == g, x, 0.0)

    def per_seg_dot(x_f32, state_ref, h):
        parts = [_dot(x_f32[g * seg:(g + 1) * seg, :].astype(BF16), state_ref[g, h].astype(BF16))
                 for g in range(nseg)]
        return parts[0] if nseg == 1 else jnp.concatenate(parts, axis=0)

    fillers = list(fillers)

    per_slot = -(-len(fillers) // MIX_STAGES)

    def next_filler():
        for _ in range(min(per_slot, len(fillers))):
            fillers.pop(0)()

    next_filler()
    zg_ref[orow, :] = zb_ref[:, Z_MIX:].astype(BF16)

    zs = zs_ref[...]
    gates = zs + bias_ref[...]
    lf = _log_sigmoid(gates)
    cum = _masked_cumsum(tril_b, lf, 3)
    cols = jnp.where(lane < HEADS, gates, cum)
    sel = (lax.broadcasted_iota(jnp.int32, (SUBLANE, LANE), 0)
           == lax.broadcasted_iota(jnp.int32, (SUBLANE, LANE), 1)).astype(F32)
    rows_ = lax.dot_general(sel, cols, (((1,), (1,)), ((), ())),
                            precision=HIGHEST, preferred_element_type=F32)

    def lhs_with_state(a_intra, x_inter):
        return jnp.concatenate([a_intra.astype(BF16), x_inter.astype(BF16)], axis=1)

    def rhs_with_state(v, state_ref, h):
        return jnp.concatenate([v, state_ref[0, h].astype(BF16)], axis=0)

    m_old = [min_[g] for g in range(nseg)]
    m_new = [jnp.zeros((1, HEADS), F32) for _ in range(nseg)]

    heads = range(HEADS)

    qf = [zb_ref[:, MQ + DK * h:MQ + DK * (h + 1)] for h in heads]
    kf = [zb_ref[:, MK + DK * h:MK + DK * (h + 1)] for h in heads]
    vm = [zb_ref[:, MV + DV * h:MV + DV * (h + 1)].astype(BF16) for h in heads]
    li_c = [cols[:, h:h + 1] for h in heads]
    b_c = [cols[:, HEADS + h:HEADS + h + 1] for h in heads]
    mt, w_inter, amat_m = [], [], []
    for h in heads:
        li_r = rows_[h:h + 1, :]
        b_r = rows_[HEADS + h:HEADS + h + 1, :]
        m_col = seg_rows([m_old[g][:, h:h + 1] for g in range(nseg)])
        dmat = jnp.where(causal, b_c[h] + (li_r - b_r), -jnp.inf)
        inter = b_c[h] + m_col
        mt.append(jnp.maximum(inter, jnp.max(dmat, axis=1, keepdims=True)))
        w_inter.append(jnp.exp(inter - mt[h]))
        scores = _dot_nt(qf[h].astype(BF16), kf[h].astype(BF16))
        amat_m.append(jnp.exp(dmat - mt[h]) * (scores * SCALE_K))
        next_filler()

    qs = [zb_ref[:, GQ + DK * h:GQ + DK * (h + 1)] for h in heads]
    ks = [zb_ref[:, GK + DK * h:GK + DK * (h + 1)] for h in heads]
    vg = [zb_ref[:, GV + DV * h:GV + DV * (h + 1)].astype(BF16) for h in heads]
    b_h = [bc[:, DK * h:DK * (h + 1)] for h in heads]
    qe = [qs[h] * jnp.exp(b_h[h]) for h in heads]

    def intra_single_anchor():
        out = []
        for h in heads:
            ke = ks[h] * jnp.exp(-b_h[h])
            out.append(jnp.where(causal, _dot_nt(qe[h].astype(BF16), ke.astype(BF16)), 0.0))
            next_filler()
        return out

    def intra_by_levels():
        out = [jnp.where(row == col, jnp.sum(qs[h] * ks[h], axis=1, keepdims=True), 0.0)
               for h in heads]
        half = 1
        while half < seg:
            wgt = jnp.exp(-jnp.abs(bc - _anchor(bc, half)))
            pair = (col < row) & (rxc >= half) & (rxc < 2 * half)
            for h in heads:
                w_h = wgt[:, DK * h:DK * (h + 1)]
                lvl = _dot_nt((qs[h] * w_h).astype(BF16), (ks[h] * w_h).astype(BF16))
                out[h] = jnp.where(pair, lvl, out[h])
            half *= 2
        for h in heads:
            next_filler()
        return out

    amat_g = intra_single_anchor() if single_anchor else intra_by_levels()

    hh = []
    for h in heads:
        n_rows = seg_rows([nin[g, h:h + 1, :] for g in range(nseg)])
        qn = jnp.sum(qf[h] * n_rows, axis=1, keepdims=True)
        if nseg == 1:
            num = _dot(lhs_with_state(amat_m[h], w_inter[h] * qf[h]), rhs_with_state(vm[h], cin, h))
        else:
            num = _dot(amat_m[h].astype(BF16), vm[h]) + w_inter[h] * per_seg_dot(qf[h], cin, h)
        den = jnp.sum(amat_m[h], axis=1, keepdims=True) + w_inter[h] * qn
        den = jnp.maximum(jnp.abs(den), jnp.exp(-mt[h]))
        hh.append(num * (1.0 / den))
        next_filler()

    og_ = []
    for h in heads:
        if nseg == 1:
            o = _dot(lhs_with_state(amat_g[h] * SCALE_K, qe[h] * SCALE_K), rhs_with_state(vg[h], sin, h))
        else:
            o = _dot((amat_g[h] * SCALE_K).astype(BF16), vg[h]) + SCALE_K * per_seg_dot(qe[h], sin, h)
        og_.append(o)
        next_filler()

    for h in heads:
        mu = jnp.mean(hh[h], axis=-1, keepdims=True)
        hc = hh[h] - mu
        hn = hc * lax.rsqrt(jnp.mean(hc * hc, axis=-1, keepdims=True) + EPS)
        og = zb_ref[:, MO + DV * h:MO + DV * (h + 1)]
        hm_ref[orow, DV * h:DV * (h + 1)] = (hn * gm_ref[:, DV * h:DV * (h + 1)] * _sigmoid(og)).astype(BF16)
        next_filler()

    for h in heads:
        o = og_[h]
        on = o * lax.rsqrt(jnp.mean(o * o, axis=-1, keepdims=True) + EPS)
        gg = zb_ref[:, GG + DV * h:GG + DV * (h + 1)]
        hg_ref[orow, DV * h:DV * (h + 1)] = (on * gg_ref[:, DV * h:DV * (h + 1)] * (gg * _sigmoid(gg))).astype(BF16)
        next_filler()

    for h in heads:
        ws = jnp.exp(seg_last(b_c[h]) - b_c[h] + li_c[h] - seg_last(mt[h])) * SCALE_K
        kw = kf[h] * ws
        for g in range(nseg):
            last = (g + 1) * seg - 1
            m_last = mt[h][last:last + 1, :]
            cw = jnp.exp(b_c[h][last:last + 1, :] + m_old[g][:, h:h + 1] - m_last)
            c_new = cw * cin[g, h] + _dot_tn(seg_mask(kw, g).astype(BF16), vm[h])
            n_new = cw * nin[g, h:h + 1, :] + jnp.sum(kw[g * seg:(g + 1) * seg, :], axis=0, keepdims=True)
            c_ref[g, h] = c_new
            n_ref[g, h:h + 1, :] = n_new
            m_new[g] = jnp.where(lane4 == h, m_last, m_new[g])
        next_filler()
    for g in range(nseg):
        m_ref[g] = m_new[g]

    bl_rows = seg_last(bc)
    for h in heads:
        kt = ks[h] * jnp.exp(bl_rows[:, DK * h:DK * (h + 1)] - b_h[h])
        for g in range(nseg):
            last = (g + 1) * seg - 1
            dec_row = jnp.exp(b_h[h][last:last + 1, :])
            dec_col = jnp.sum(jnp.where(eye_k, jnp.broadcast_to(dec_row, (DK, DK)), 0.0),
                              axis=1, keepdims=True)
            s_ref[g, h] = dec_col * sin[g, h] + _dot_tn(seg_mask(kt, g).astype(BF16), vg[h])
        next_filler()
    assert not fillers


def _front_kernel(*refs, L, seg, pipelined, carry, aliased):
    lead = 5 if pipelined else 2
    pos = lead
    c0_ref, n0_ref, m0_ref, s0_ref = refs[pos:pos + 4]
    pos += 4
    params = refs[pos:pos + 5]
    pos += 5 + (4 if aliased else 0)
    outs = refs[pos:pos + 7]
    pos += 7
    scratch = refs[pos:]
    c_ref, n_ref, m_ref, s_ref = outs[3:]

    if not pipelined:
        z_ref, zs_ref = refs[:lead]
        state_in = (c0_ref, n0_ref, m0_ref, s0_ref)
        if carry:
            @pl.when(pl.program_id(1) == 0)
            def _():
                c_ref[...] = c0_ref[...]
                n_ref[...] = n0_ref[...]
                m_ref[...] = m0_ref[...]
                s_ref[...] = s0_ref[...]
            state_in = (c_ref, n_ref, m_ref, s_ref)
        _mix_chunk(z_ref, zs_ref, 0, state_in, params, outs, L=L, seg=seg)
        return

    h_cur, h_nxt, g_ref, wbig_ref, wsm_ref = refs[:lead]
    stages = functools.partial(_project_stages, g_ref=g_ref, wbig_ref=wbig_ref, wsm_ref=wsm_ref)
    z_a, zs_a, z_b, zs_b = scratch

    @pl.when(pl.program_id(1) == 0)
    def _():
        c_ref[...] = c0_ref[...]
        n_ref[...] = n0_ref[...]
        m_ref[...] = m0_ref[...]
        s_ref[...] = s0_ref[...]
        for stage in stages(h_cur, 0, L, z_ref=z_a, zs_ref=zs_a):
            stage()

    carried = (c_ref, n_ref, m_ref, s_ref)
    _mix_chunk(z_a, zs_a, 0, carried, params, outs, L=L, seg=seg,
               make_fillers=lambda: stages(h_cur, L, L, z_ref=z_b, zs_ref=zs_b))
    _mix_chunk(z_b, zs_b, L, carried, params, outs, L=L, seg=seg,
               make_fillers=lambda: stages(h_nxt, 0, L, z_ref=z_a, zs_ref=zs_a))


def _front(h, states_in, in_layer, states_out, layer, W, *, L, seg, seq_rows):
    T = h.shape[0]
    nseg = L // seg
    B = states_in[0].shape[1]
    pipelined = FUSE_PROJECTION and seq_rows > L
    rows = 2 * L if pipelined else L
    steps = max(seq_rows // rows, 1)
    grid = (B // nseg, steps)
    row_map = lambda b, k: (b * steps + k, 0)
    nxt_map = lambda b, k: (2 * (b * steps + jnp.minimum(k + 1, steps - 1)), 0)
    par = lambda b, k: (layer, 0, 0)

    def state_specs(at):
        return [
            pl.BlockSpec((None, nseg, HEADS, DK, DV), lambda b, k: (at, b, 0, 0, 0)),
            pl.BlockSpec((None, nseg, HEADS, DK), lambda b, k: (at, b, 0, 0)),
            pl.BlockSpec((None, nseg, 1, HEADS), lambda b, k: (at, b, 0, 0)),
            pl.BlockSpec((None, nseg, HEADS, DK, DV), lambda b, k: (at, b, 0, 0, 0)),
        ]

    if pipelined:
        in_specs = [
            pl.BlockSpec((rows, D_MODEL), row_map),
            pl.BlockSpec((L, D_MODEL), nxt_map),
            _resident((None, 1, D_MODEL), par),
            _resident((None, D_MODEL, Z_BIG), par),
            _resident((None, D_MODEL, Z_SMALL), par),
        ]
        args = [h, h, W["norm_mix_g"], W["w_big"], W["w_small"]]
    else:
        in_specs = [pl.BlockSpec((rows, Z_BIG), row_map), pl.BlockSpec((rows, Z_SMALL), row_map)]
        args = list(_in_proj(h, layer, W))
    in_specs += [
        *state_specs(in_layer),
        _resident((None, 1, Z_SMALL), par),
        _resident((None, Z_SMALL, HEADS * DK), par),
        _resident((None, 1, HEADS * DK), par),
        _resident((None, 1, HEADS * DV), par),
        _resident((None, 1, HEADS * DV), par),
    ]
    args += [*states_in, W["gate_bias"], W["w_lr2"], W["b_lr2"], W["g_mnorm"], W["g_gnorm"]]
    aliases = {}
    if states_out is not None:
        for n, s in enumerate(states_out):
            aliases[len(args)] = 3 + n
            in_specs.append(pl.BlockSpec(memory_space=pl.ANY))
            args.append(s)
    out_shape = [
        jax.ShapeDtypeStruct((T, HEADS * DV), BF16),
        jax.ShapeDtypeStruct((T, HEADS * DV), BF16),
        jax.ShapeDtypeStruct((T, Z_GATES), BF16),
        jax.ShapeDtypeStruct((DEPTH, B, HEADS, DK, DV), F32),
        jax.ShapeDtypeStruct((DEPTH, B, HEADS, DK), F32),
        jax.ShapeDtypeStruct((DEPTH, B, 1, HEADS), F32),
        jax.ShapeDtypeStruct((DEPTH, B, HEADS, DK, DV), F32),
    ]
    z_scratch = [pltpu.VMEM((L, Z_BIG), F32), pltpu.VMEM((L, Z_SMALL), F32)]
    outs = pl.pallas_call(
        functools.partial(_front_kernel, L=L, seg=seg, pipelined=pipelined, carry=steps > 1,
                          aliased=states_out is not None),
        grid=grid,
        in_specs=in_specs,
        out_specs=[
            pl.BlockSpec((rows, HEADS * DV), row_map),
            pl.BlockSpec((rows, HEADS * DV), row_map),
            pl.BlockSpec((rows, Z_GATES), row_map),
            *state_specs(layer),
        ],
        out_shape=out_shape,
        input_output_aliases=aliases,
        scratch_shapes=z_scratch * 2 if pipelined else [],
        compiler_params=pltpu.CompilerParams(
            dimension_semantics=("arbitrary", "arbitrary"),
            vmem_limit_bytes=VMEM_LIMIT_BYTES,
        ),
        name="front",
    )(*args)
    return outs[0], outs[1], outs[2], tuple(outs[3:])


def _channel_kernel(*refs, seq_tiles, has_state, final):
    (h_ref, hm_ref, hg_ref, ga_ref, gb_ref, wpa_ref, wpb_ref, wo_ref, gffn_ref,
     wup_ref, cw_ref, cb_ref, wd_ref, gple_ref, wpg_ref, wpp_ref, p_ref) = refs[:17]
    pos = 17
    pst_ref = gfin_ref = carry_scr = None
    if has_state:
        pst_ref = refs[pos]
        pos += 1
    if final:
        gfin_ref = refs[pos]
        pos += 1
    out_ref, conv_ref = refs[pos:pos + 2]
    if not has_state:
        carry_scr = refs[pos + 2]

    i = pl.program_id(0)
    tm = h_ref.shape[0]
    hr = tm // CHAN_SPLIT
    groups = [slice(r * hr, (r + 1) * hr) for r in range(CHAN_SPLIT)]

    pa = [_dot(hm_ref[rs, :], wpa_ref[...]) for rs in groups]
    pb = [_dot(hg_ref[rs, :], wpb_ref[...]) for rs in groups]
    h1, c = [], []
    for r, rs in enumerate(groups):
        merged = (_sigmoid(ga_ref[rs, :].astype(F32)) * pa[r]
                  + _sigmoid(gb_ref[rs, :].astype(F32)) * pb[r])
        h1.append(h_ref[rs, :] + _dot(merged.astype(BF16), wo_ref[...]))
        c.append(_rmsnorm(h1[r], gffn_ref[...]).astype(BF16))

    up = [[None] * CHAN_SPLIT for _ in range(2 * D_FF // UP_CHUNK)]
    for n in range(2 * D_FF // UP_CHUNK):
        for r in range(CHAN_SPLIT):
            up[n][r] = _dot(c[r], wup_ref[:, n * UP_CHUNK:(n + 1) * UP_CHUNK])

    def up_cols(start, r):
        n, off = divmod(start, UP_CHUNK)
        assert off + FF_CHUNK <= UP_CHUNK
        return up[n][r][:, off:off + FF_CHUNK]

    rowi = lax.broadcasted_iota(jnp.int32, (hr, FF_CHUNK), 0)
    acc = [None] * CHAN_SPLIT
    for n in range(D_FF // FF_CHUNK):
        ff = slice(n * FF_CHUNK, (n + 1) * FF_CHUNK)
        if not has_state:
            prev = jnp.where(i % seq_tiles == 0, 0.0, carry_scr[:, ff])
        for r, rs in enumerate(groups):
            ug = up_cols(n * FF_CHUNK, r)
            uv = up_cols(D_FF + n * FF_CHUNK, r)
            r1 = pltpu.roll(ug, 1, 0)
            r2 = pltpu.roll(ug, 2, 0)
            if has_state:
                pst = pst_ref[rs, ff]
                t = rowi & (SUBLANE - 1)
                u1 = jnp.where(t == 0, pltpu.roll(pst, hr - 1, 0), r1)
                u2 = jnp.where(t < 2, pst, r2)
                conv_ref[rs, ff] = ug
            else:
                p6 = jnp.broadcast_to(prev[6:7, :], ug.shape)
                p7 = jnp.broadcast_to(prev[7:8, :], ug.shape)
                u1 = jnp.where(rowi == 0, p7, r1)
                u2 = jnp.where(rowi == 0, p6, jnp.where(rowi == 1, p7, r2))
                prev = ug[hr - SUBLANE:, :]
            conv = cb_ref[:, ff] + cw_ref[0:1, ff] * u2 + cw_ref[1:2, ff] * u1 + cw_ref[2:3, ff] * ug
            act = conv * _sigmoid(conv) * uv
            part = _dot(act.astype(BF16), wd_ref[ff, :])
            acc[r] = part if acc[r] is None else acc[r] + part
        if not has_state:
            carry_scr[:, ff] = prev
            conv_ref[:, ff] = prev

    for r, rs in enumerate(groups):
        h2 = h1[r] + acc[r]
        e = _rmsnorm(h2, gple_ref[...]).astype(BF16)
        h3 = h2 + _sigmoid(_dot(e, wpg_ref[...])) * _dot(p_ref[rs, :].astype(BF16), wpp_ref[...])
        if final:
            h3 = _rmsnorm(h3, gfin_ref[...])
        out_ref[rs, :] = h3


def _channel(h, hm, hg, zg, p, pst, layer, W, *, seq_rows, final):
    T = h.shape[0]
    tm = CHAN_ROWS
    has_state = pst is not None
    seq_tiles = max(seq_rows // tm, 1)
    rows = lambda i: (i, 0)
    par = lambda i: (layer, 0, 0)
    in_specs = [
        pl.BlockSpec((tm, D_MODEL), rows),
        pl.BlockSpec((tm, D_MODEL), rows),
        pl.BlockSpec((tm, D_MODEL), rows),
        pl.BlockSpec((tm, D_MODEL), lambda i: (i, (GA - Z_MIX) // D_MODEL)),
        pl.BlockSpec((tm, D_MODEL), lambda i: (i, (GB - Z_MIX) // D_MODEL)),
        _resident((None, D_MODEL, D_MODEL), par),
        _resident((None, D_MODEL, D_MODEL), par),
        _resident((None, D_MODEL, D_MODEL), par),
        _resident((None, 1, D_MODEL), par),
        _resident((None, D_MODEL, 2 * D_FF), par),
        _resident((None, CONV_W, D_FF), par),
        _resident((None, 1, D_FF), par),
        _resident((None, D_FF, D_MODEL), par),
        _resident((None, 1, D_MODEL), par),
        _resident((None, D_MODEL, D_MODEL), par),
        _resident((None, PLE_DIM, D_MODEL), par),
        pl.BlockSpec((None, tm, PLE_DIM), lambda i: (layer, i, 0)),
    ]
    args = [h, hm, hg, zg, zg, W["w_pa"], W["w_pb"], W["w_o"], W["norm_ffn_g"],
            W["w_up"], W["conv_w"], W["conv_b"], W["w_down"], W["norm_ple_g"],
            W["w_ple_gate"], W["w_ple_proj"], p]
    scratch = []
    if has_state:
        in_specs.append(pl.BlockSpec((tm, D_FF), rows))
        args.append(pst)
        conv_spec = pl.BlockSpec((tm, D_FF), rows)
        conv_shape = jax.ShapeDtypeStruct((T, D_FF), F32)
    else:
        conv_spec = pl.BlockSpec((None, SUBLANE, D_FF), lambda i: (i, 0, 0))
        conv_shape = jax.ShapeDtypeStruct((T // tm, SUBLANE, D_FF), F32)
        scratch.append(pltpu.VMEM((SUBLANE, D_FF), F32))
    if final:
        in_specs.append(_resident((1, D_MODEL), lambda i: (0, 0)))
        args.append(W["final_norm_g"])
    return pl.pallas_call(
        functools.partial(_channel_kernel, seq_tiles=seq_tiles, has_state=has_state, final=final),
        grid=(T // tm,),
        in_specs=in_specs,
        out_specs=[pl.BlockSpec((tm, D_MODEL), rows), conv_spec],
        out_shape=[jax.ShapeDtypeStruct((T, D_MODEL), F32), conv_shape],
        scratch_shapes=scratch,
        compiler_params=pltpu.CompilerParams(
            dimension_semantics=("arbitrary",),
            vmem_limit_bytes=VMEM_LIMIT_BYTES,
        ),
        name="channel",
    )(*args)


def _prep_weights(norm_mix_g, w_in, b_mi, b_mf, g_mnorm, w_lr2, b_lr2, g_gnorm, w_pa, w_pb, w_o,
                  norm_ffn_g, w_up, conv_w, conv_b, w_down, norm_ple_g, w_ple_gate, w_ple_proj,
                  final_norm_g):
    gate_bias = jnp.concatenate([b_mi, b_mf, jnp.zeros((DEPTH, Z_SMALL - 2 * HEADS), F32)], axis=-1)
    w_lr2_pad = jnp.zeros((DEPTH, Z_SMALL, HEADS * DK), F32).at[:, LR_LANE:LR_LANE + G_RANK, :].set(w_lr2)
    row = lambda x: x.reshape(DEPTH, 1, x.shape[-1])
    w_in_t = jnp.swapaxes(w_in, 1, 2)
    return dict(
        norm_mix_g=row(norm_mix_g), w_big=_repack_w_in(w_in_t), w_small=_narrow_w_in(w_in_t),
        gate_bias=row(gate_bias), w_lr2=w_lr2_pad.astype(BF16), b_lr2=row(b_lr2),
        g_mnorm=row(g_mnorm), g_gnorm=row(g_gnorm),
        w_pa=w_pa.astype(BF16), w_pb=w_pb.astype(BF16), w_o=w_o.astype(BF16),
        norm_ffn_g=row(norm_ffn_g), w_up=w_up.astype(BF16), conv_w=conv_w, conv_b=row(conv_b),
        w_down=w_down.astype(BF16), norm_ple_g=row(norm_ple_g),
        w_ple_gate=w_ple_gate.astype(BF16), w_ple_proj=w_ple_proj.astype(BF16),
        final_norm_g=final_norm_g.reshape(1, D_MODEL),
    )


def _trunk(x, p, states, stacked, cv0, W, *, seq_rows):
    T = x.shape[0]
    B = T // seq_rows
    if seq_rows >= 2 * PROMPT_CHUNK:
        L, seg = PROMPT_CHUNK, PROMPT_CHUNK
    else:
        L, seg = SAMPLE_GROUP * seq_rows, seq_rows
    h = x
    new_states = None
    cvs = []
    for layer in range(DEPTH):
        hm, hg, zg, new_states = _front(h, states, layer if stacked else 0, new_states, layer, W,
                                        L=L, seg=seg, seq_rows=seq_rows)
        if cv0 is None:
            pst = None
        else:
            pst = jnp.pad(cv0[layer], ((0, 0), (0, seq_rows - (CONV_W - 1)), (0, 0))).reshape(T, D_FF)
        h, cv = _channel(h, hm, hg, zg, p, pst, layer, W, seq_rows=seq_rows, final=layer == DEPTH - 1)
        if cv0 is None:
            cv = cv.reshape(B, -1, SUBLANE, D_FF)[:, -1, SUBLANE - (CONV_W - 1):, :]
        else:
            cv = cv.reshape(B, seq_rows, D_FF)[:, seq_rows - (CONV_W - 1):, :]
        cvs.append(cv)
    c, n, m, s = new_states
    return h, c, n, m.reshape(DEPTH, B, HEADS), s, jnp.stack(cvs)


def kernel(x_prompt, x_sample, state_mlstm_C, state_mlstm_n, state_mlstm_m, state_gla_S, state_ffn_conv, p_prompt, p_sample, norm_mix_g, w_in, b_mi, b_mf, g_mnorm, w_lr2, b_lr2, g_gnorm, w_pa, w_pb, w_o, norm_ffn_g, w_up, conv_w, conv_b, w_down, norm_ple_g, w_ple_gate, w_ple_proj, final_norm_g):
    W = _prep_weights(norm_mix_g, w_in, b_mi, b_mf, g_mnorm, w_lr2, b_lr2, g_gnorm, w_pa, w_pb, w_o,
                      norm_ffn_g, w_up, conv_w, conv_b, w_down, norm_ple_g, w_ple_gate, w_ple_proj,
                      final_norm_g)
    bp, sp, _ = x_prompt.shape
    bs, ss, _ = x_sample.shape
    zero_states = (jnp.zeros((1, bp, HEADS, DK, DV), F32), jnp.zeros((1, bp, HEADS, DK), F32),
                   jnp.zeros((1, bp, 1, HEADS), F32), jnp.zeros((1, bp, HEADS, DK, DV), F32))
    yp, c_p, n_p, m_p, s_p, cv_p = _trunk(
        x_prompt.reshape(bp * sp, D_MODEL), p_prompt.reshape(DEPTH, bp * sp, PLE_DIM),
        zero_states, False, None, W, seq_rows=sp)
    sample_states = (state_mlstm_C, state_mlstm_n, state_mlstm_m.reshape(DEPTH, bs, 1, HEADS), state_gla_S)
    ys, c_s, n_s, m_s, s_s, cv_s = _trunk(
        x_sample.reshape(bs * ss, D_MODEL), p_sample.reshape(DEPTH, bs * ss, PLE_DIM),
        sample_states, True, state_ffn_conv, W, seq_rows=ss)
    return (yp.reshape(bp, sp, D_MODEL), ys.reshape(bs, ss, D_MODEL),
            c_p, n_p, m_p, s_p, cv_p, c_s, n_s, m_s, s_s, cv_s)
```

```python
import functools

import jax
import jax.numpy as jnp
from jax import lax
from jax.experimental import pallas as pl
from jax.experimental.pallas import tpu as pltpu

F32 = jnp.float32
BF16 = jnp.bfloat16
HIGHEST = lax.Precision.HIGHEST

D_MODEL = 1024
DEPTH = 4
HEADS = 4
DK = 128
DV = 256
G_RANK = 16
G_TAU = 16.0
D_FF = 2688
CONV_W = 3
PLE_DIM = 256
EPS = 1e-6
SCALE_K = DK ** -0.5

MQ, MK, MV, MO = 0, 512, 1024, 2048
GQ, GK, GV, GG = 3072, 3584, 4096, 5120
GA, GB = 6144, 7168
Z_BIG = 8192
Z_MIX = 6144
Z_GATES = Z_BIG - Z_MIX
Z_SMALL = 128
LR_LANE = 8
SRC_MI, SRC_GLR = 2048, 5128
SHIFT_AFTER_MI = 2 * HEADS
SHIFT_AFTER_GLR = 2 * HEADS + G_RANK

LANE = 128
SUBLANE = 8
VMEM_LIMIT_BYTES = 60 * 1024 * 1024

PROJ_ROWS = 512
PROJ_COLS = 256
CHAN_ROWS = 256
CHAN_SPLIT = 2
UP_CHUNK = 1792
FF_CHUNK = 896
REPACK_COLS = 1024
PROMPT_CHUNK = 256
FUSE_PROJECTION = True
SAMPLE_GROUP = 4
GLA_SAFE_LOG_DECAY = -60.0


def _dot(a, b):
    return jnp.dot(a, b, preferred_element_type=F32)


def _dot_nt(a, b):
    return lax.dot_general(a, b, (((1,), (1,)), ((), ())), preferred_element_type=F32)


def _dot_tn(a, b):
    return lax.dot_general(a, b, (((0,), (0,)), ((), ())), preferred_element_type=F32)


def _log_sigmoid(x):
    return jnp.minimum(x, 0.0) - jnp.log(1.0 + jnp.exp(-jnp.abs(x)))


def _masked_cumsum(mask_bf16, x, pieces):
    acc = None
    rest = x
    for n in range(pieces):
        piece = rest.astype(BF16)
        part = _dot(mask_bf16, piece)
        acc = part if acc is None else acc + part
        if n + 1 < pieces:
            rest = rest - piece.astype(F32)
    return acc


def _sigmoid(x):
    return 0.5 * jnp.tanh(0.5 * x) + 0.5


def _rmsnorm(x, g):
    return x * lax.rsqrt(jnp.mean(x * x, axis=-1, keepdims=True) + EPS) * g


def _resident(block_shape, index_map):
    return pl.BlockSpec(block_shape, index_map, pipeline_mode=pl.Buffered(1))


REPACK_TAIL = 32


def _repack_kernel(a_ref, b_ref, o_ref):
    t = pl.program_id(1)
    first_after_mi = MO // REPACK_COLS
    first_after_glr = GG // REPACK_COLS

    def emit(shift):
        if shift == 0:
            y = a_ref[...]
        else:
            x = jnp.concatenate([a_ref[...], b_ref[...]], axis=0)
            y = x[shift:shift + REPACK_COLS, :]
        o_ref[...] = y.T.astype(BF16)

    @pl.when(t < first_after_mi)
    def _():
        emit(0)

    @pl.when((t >= first_after_mi) & (t < first_after_glr))
    def _():
        emit(SHIFT_AFTER_MI)

    @pl.when(t >= first_after_glr)
    def _():
        emit(SHIFT_AFTER_GLR)


def _repack_w_in(w_in_t):
    tails_per_tile = REPACK_COLS // REPACK_TAIL
    return pl.pallas_call(
        _repack_kernel,
        grid=(DEPTH, Z_BIG // REPACK_COLS),
        in_specs=[
            pl.BlockSpec((None, REPACK_COLS, D_MODEL), lambda l, t: (l, t, 0)),
            pl.BlockSpec((None, REPACK_TAIL, D_MODEL), lambda l, t: (l, tails_per_tile * (t + 1), 0)),
        ],
        out_specs=pl.BlockSpec((None, D_MODEL, REPACK_COLS), lambda l, t: (l, 0, t)),
        out_shape=jax.ShapeDtypeStruct((DEPTH, D_MODEL, Z_BIG), BF16),
        compiler_params=pltpu.CompilerParams(
            dimension_semantics=("arbitrary", "arbitrary"),
            vmem_limit_bytes=VMEM_LIMIT_BYTES,
        ),
        name="repack_w_in",
    )(w_in_t, w_in_t)


def _narrow_kernel(mi_ref, glr0_ref, glr1_ref, o_ref):
    pad = jnp.zeros((Z_SMALL - SHIFT_AFTER_GLR, D_MODEL), F32)
    x = jnp.concatenate([mi_ref[...], glr0_ref[...], glr1_ref[...], pad], axis=0)
    o_ref[...] = x.T.astype(BF16)


def _narrow_w_in(w_in_t):
    assert SRC_MI % SUBLANE == 0 and SRC_GLR % SUBLANE == 0 and G_RANK == 2 * SUBLANE
    rows8 = lambda r: pl.BlockSpec((None, SUBLANE, D_MODEL), lambda l: (l, r // SUBLANE, 0))
    return pl.pallas_call(
        _narrow_kernel,
        grid=(DEPTH,),
        in_specs=[rows8(SRC_MI), rows8(SRC_GLR), rows8(SRC_GLR + SUBLANE)],
        out_specs=pl.BlockSpec((None, D_MODEL, Z_SMALL), lambda l: (l, 0, 0)),
        out_shape=jax.ShapeDtypeStruct((DEPTH, D_MODEL, Z_SMALL), BF16),
        compiler_params=pltpu.CompilerParams(dimension_semantics=("arbitrary",)),
        name="narrow_w_in",
    )(w_in_t, w_in_t, w_in_t)


def _in_proj_kernel(h_ref, g_ref, wbig_ref, wsm_ref, zbig_ref, zsm_ref):
    a = _rmsnorm(h_ref[...], g_ref[...]).astype(BF16)
    zsm_ref[...] = _dot(a, wsm_ref[...])
    for n in range(Z_BIG // PROJ_COLS):
        cols = slice(n * PROJ_COLS, (n + 1) * PROJ_COLS)
        zbig_ref[:, cols] = _dot(a, wbig_ref[:, cols])


def _in_proj(h, layer, W):
    T = h.shape[0]
    par = lambda i: (layer, 0, 0)
    return pl.pallas_call(
        _in_proj_kernel,
        grid=(T // PROJ_ROWS,),
        in_specs=[
            pl.BlockSpec((PROJ_ROWS, D_MODEL), lambda i: (i, 0)),
            _resident((None, 1, D_MODEL), par),
            _resident((None, D_MODEL, Z_BIG), par),
            _resident((None, D_MODEL, Z_SMALL), par),
        ],
        out_specs=[
            pl.BlockSpec((PROJ_ROWS, Z_BIG), lambda i: (i, 0)),
            pl.BlockSpec((PROJ_ROWS, Z_SMALL), lambda i: (i, 0)),
        ],
        out_shape=[
            jax.ShapeDtypeStruct((T, Z_BIG), F32),
            jax.ShapeDtypeStruct((T, Z_SMALL), F32),
        ],
        compiler_params=pltpu.CompilerParams(
            dimension_semantics=("arbitrary",),
            vmem_limit_bytes=VMEM_LIMIT_BYTES,
        ),
        name="in_proj",
    )(h, W["norm_mix_g"], W["w_big"], W["w_small"])


N_PROJ_STAGES = 1 + Z_BIG // PROJ_COLS
MIX_STAGES = 1 + 8 * HEADS


def _project_stages(h_ref, r0, rows, g_ref, wbig_ref, wsm_ref, z_ref, zs_ref):
    held = {}

    def head():
        a = _rmsnorm(h_ref[r0:r0 + rows, :], g_ref[...]).astype(BF16)
        held["a"] = a
        zs_ref[...] = _dot(a, wsm_ref[...])

    def body(n):
        cols = slice(n * PROJ_COLS, (n + 1) * PROJ_COLS)
        z_ref[:, cols] = _dot(held["a"], wbig_ref[:, cols])

    return [head] + [functools.partial(body, n) for n in range(Z_BIG // PROJ_COLS)]


def _anchor(x, half):
    rows, width = x.shape
    if half >= SUBLANE:
        parts = []
        for blk in range(rows // (2 * half)):
            r = blk * 2 * half + half - 1
            parts.append(jnp.broadcast_to(x[r:r + 1, :], (2 * half, width)))
        return parts[0] if len(parts) == 1 else jnp.concatenate(parts, axis=0)
    x3 = x.reshape(rows // SUBLANE, SUBLANE, width)
    sub = lax.broadcasted_iota(jnp.int32, x3.shape, 1)

    def bc(i):
        return jnp.broadcast_to(x3[:, i:i + 1, :], x3.shape)

    if half == 4:
        out = bc(3)
    elif half == 2:
        out = jnp.where(sub < 4, bc(1), bc(5))
    else:
        out = jnp.where(sub < 2, bc(0), jnp.where(sub < 4, bc(2), jnp.where(sub < 6, bc(4), bc(6))))
    return out.reshape(rows, width)


def _causal_mask(L, seg):
    row = lax.broadcasted_iota(jnp.int32, (L, L), 0)
    col = lax.broadcasted_iota(jnp.int32, (L, L), 1)
    return (col <= row) & ((row ^ col) < seg)


def _mix_chunk(zb_ref, zs_ref, r0, state_in, params, outs, *, L, seg, make_fillers=None):
    w2_ref, blr_ref = params[1:3]
    tril_b = _causal_mask(L, seg).astype(F32).astype(BF16)
    la = _log_sigmoid(_dot(zs_ref[...].astype(BF16), w2_ref[...]) + blr_ref[...]) * (1.0 / G_TAU)
    bc = _masked_cumsum(tril_b, la, 2)

    def run(single_anchor):
        fillers = make_fillers() if make_fillers else ()
        _mix_rest(zb_ref, zs_ref, bc, r0, state_in, params, outs, L=L, seg=seg,
                  single_anchor=single_anchor, fillers=fillers)

    lax.cond(jnp.min(bc) >= GLA_SAFE_LOG_DECAY, lambda: run(True), lambda: run(False))


def _mix_rest(zb_ref, zs_ref, bc, r0, state_in, params, outs, *, L, seg, single_anchor, fillers):
    cin, nin, min_, sin = state_in
    bias_ref, w2_ref, blr_ref, gm_ref, gg_ref = params
    hm_ref, hg_ref, zg_ref, c_ref, n_ref, m_ref, s_ref = outs
    orow = slice(r0, r0 + L)
    nseg = L // seg
    seg_shift = seg.bit_length() - 1

    row = lax.broadcasted_iota(jnp.int32, (L, L), 0)
    col = lax.broadcasted_iota(jnp.int32, (L, L), 1)
    rxc = row ^ col
    causal = (col <= row) & (rxc < seg)
    tril_b = causal.astype(F32).astype(BF16)
    lane = lax.broadcasted_iota(jnp.int32, (L, LANE), 1)
    rowl = lax.broadcasted_iota(jnp.int32, (L, LANE), 0)
    eye_k = (lax.broadcasted_iota(jnp.int32, (DK, DK), 0)
             == lax.broadcasted_iota(jnp.int32, (DK, DK), 1))
    lane4 = lax.broadcasted_iota(jnp.int32, (1, HEADS), 1)

    def seg_last(x):
        parts = [jnp.broadcast_to(x[(g + 1) * seg - 1:(g + 1) * seg, :], (seg, x.shape[1]))
                 for g in range(nseg)]
        return parts[0] if nseg == 1 else jnp.concatenate(parts, axis=0)

    def seg_rows(vals):
        parts = [jnp.broadcast_to(v, (seg, v.shape[1])) for v in vals]
        return parts[0] if nseg == 1 else jnp.concatenate(parts, axis=0)

    def seg_mask(x, g):
        return x if nseg == 1 else jnp.where((rowl >> seg_shift) == g, x, 0.0)

    def per_seg_dot(x_f32, state_ref, h):
        parts = [_dot(x_f32[g * seg:(g + 1) * seg, :].astype(BF16), state_ref[g, h].astype(BF16))
                 for g in range(nseg)]
        return parts[0] if nseg == 1 else jnp.concatenate(parts, axis=0)

    fillers = list(fillers)

    per_slot = -(-len(fillers) // MIX_STAGES)

    def next_filler():
        for _ in range(min(per_slot, len(fillers))):
            fillers.pop(0)()

    next_filler()
    zg_ref[orow, :] = zb_ref[:, Z_MIX:].astype(BF16)

    zs = zs_ref[...]
    gates = zs + bias_ref[...]
    lf = _log_sigmoid(gates)
    cum = _masked_cumsum(tril_b, lf, 3)
    cols = jnp.where(lane < HEADS, gates, cum)
    sel = (lax.broadcasted_iota(jnp.int32, (SUBLANE, LANE), 0)
           == lax.broadcasted_iota(jnp.int32, (SUBLANE, LANE), 1)).astype(F32)
    rows_ = lax.dot_general(sel, cols, (((1,), (1,)), ((), ())),
                            precision=HIGHEST, preferred_element_type=F32)

    def lhs_with_state(a_intra, x_inter):
        return jnp.concatenate([a_intra.astype(BF16), x_inter.astype(BF16)], axis=1)

    def rhs_with_state(v, state_ref, h):
        return jnp.concatenate([v, state_ref[0, h].astype(BF16)], axis=0)

    m_old = [min_[g] for g in range(nseg)]
    m_new = [jnp.zeros((1, HEADS), F32) for _ in range(nseg)]

    heads = range(HEADS)

    qf = [zb_ref[:, MQ + DK * h:MQ + DK * (h + 1)] for h in heads]
    kf = [zb_ref[:, MK + DK * h:MK + DK * (h + 1)] for h in heads]
    vm = [zb_ref[:, MV + DV * h:MV + DV * (h + 1)].astype(BF16) for h in heads]
    li_c = [cols[:, h:h + 1] for h in heads]
    b_c = [cols[:, HEADS + h:HEADS + h + 1] for h in heads]
    mt, w_inter, amat_m = [], [], []
    for h in heads:
        li_r = rows_[h:h + 1, :]
        b_r = rows_[HEADS + h:HEADS + h + 1, :]
        m_col = seg_rows([m_old[g][:, h:h + 1] for g in range(nseg)])
        dmat = jnp.where(causal, b_c[h] + (li_r - b_r), -jnp.inf)
        inter = b_c[h] + m_col
        mt.append(jnp.maximum(inter, jnp.max(dmat, axis=1, keepdims=True)))
        w_inter.append(jnp.exp(inter - mt[h]))
        scores = _dot_nt(qf[h].astype(BF16), kf[h].astype(BF16))
        amat_m.append(jnp.exp(dmat - mt[h]) * (scores * SCALE_K))
        next_filler()

    qs = [zb_ref[:, GQ + DK * h:GQ + DK * (h + 1)] for h in heads]
    ks = [zb_ref[:, GK + DK * h:GK + DK * (h + 1)] for h in heads]
    vg = [zb_ref[:, GV + DV * h:GV + DV * (h + 1)].astype(BF16) for h in heads]
    b_h = [bc[:, DK * h:DK * (h + 1)] for h in heads]
    qe = [qs[h] * jnp.exp(b_h[h]) for h in heads]

    def intra_single_anchor():
        out = []
        for h in heads:
            ke = ks[h] * jnp.exp(-b_h[h])
            out.append(jnp.where(causal, _dot_nt(qe[h].astype(BF16), ke.astype(BF16)), 0.0))
            next_filler()
        return out

    def intra_by_levels():
        out = [jnp.where(row == col, jnp.sum(qs[h] * ks[h], axis=1, keepdims=True), 0.0)
               for h in heads]
        half = 1
        while half < seg:
            wgt = jnp.exp(-jnp.abs(bc - _anchor(bc, half)))
            pair = (col < row) & (rxc >= half) & (rxc < 2 * half)
            for h in heads:
                w_h = wgt[:, DK * h:DK * (h + 1)]
                lvl = _dot_nt((qs[h] * w_h).astype(BF16), (ks[h] * w_h).astype(BF16))
                out[h] = jnp.where(pair, lvl, out[h])
            half *= 2
        for h in heads:
            next_filler()
        return out

    amat_g = intra_single_anchor() if single_anchor else intra_by_levels()

    hh = []
    for h in heads:
        n_rows = seg_rows([nin[g, h:h + 1, :] for g in range(nseg)])
        qn = jnp.sum(qf[h] * n_rows, axis=1, keepdims=True)
        if nseg == 1:
            num = _dot(lhs_with_state(amat_m[h], w_inter[h] * qf[h]), rhs_with_state(vm[h], cin, h))
        else:
            num = _dot(amat_m[h].astype(BF16), vm[h]) + w_inter[h] * per_seg_dot(qf[h], cin, h)
        den = jnp.sum(amat_m[h], axis=1, keepdims=True) + w_inter[h] * qn
        den = jnp.maximum(jnp.abs(den), jnp.exp(-mt[h]))
        hh.append(num * (1.0 / den))
        next_filler()

    og_ = []
    for h in heads:
        if nseg == 1:
            o = _dot(lhs_with_state(amat_g[h] * SCALE_K, qe[h] * SCALE_K), rhs_with_state(vg[h], sin, h))
        else:
            o = _dot((amat_g[h] * SCALE_K).astype(BF16), vg[h]) + SCALE_K * per_seg_dot(qe[h], sin, h)
        og_.append(o)
        next_filler()

    for h in heads:
        mu = jnp.mean(hh[h], axis=-1, keepdims=True)
        hc = hh[h] - mu
        hn = hc * lax.rsqrt(jnp.mean(hc * hc, axis=-1, keepdims=True) + EPS)
        og = zb_ref[:, MO + DV * h:MO + DV * (h + 1)]
        hm_ref[orow, DV * h:DV * (h + 1)] = (hn * gm_ref[:, DV * h:DV * (h + 1)] * _sigmoid(og)).astype(BF16)
        next_filler()

    for h in heads:
        o = og_[h]
        on = o * lax.rsqrt(jnp.mean(o * o, axis=-1, keepdims=True) + EPS)
        gg = zb_ref[:, GG + DV * h:GG + DV * (h + 1)]
        hg_ref[orow, DV * h:DV * (h + 1)] = (on * gg_ref[:, DV * h:DV * (h + 1)] * (gg * _sigmoid(gg))).astype(BF16)
        next_filler()

    for h in heads:
        ws = jnp.exp(seg_last(b_c[h]) - b_c[h] + li_c[h] - seg_last(mt[h])) * SCALE_K
        kw = kf[h] * ws
        for g in range(nseg):
            last = (g + 1) * seg - 1
            m_last = mt[h][last:last + 1, :]
            cw = jnp.exp(b_c[h][last:last + 1, :] + m_old[g][:, h:h + 1] - m_last)
            c_new = cw * cin[g, h] + _dot_tn(seg_mask(kw, g).astype(BF16), vm[h])
            n_new = cw * nin[g, h:h + 1, :] + jnp.sum(kw[g * seg:(g + 1) * seg, :], axis=0, keepdims=True)
            c_ref[g, h] = c_new
            n_ref[g, h:h + 1, :] = n_new
            m_new[g] = jnp.where(lane4 == h, m_last, m_new[g])
        next_filler()
    for g in range(nseg):
        m_ref[g] = m_new[g]

    bl_rows = seg_last(bc)
    for h in heads:
        kt = ks[h] * jnp.exp(bl_rows[:, DK * h:DK * (h + 1)] - b_h[h])
        for g in range(nseg):
            last = (g + 1) * seg - 1
            dec_row = jnp.exp(b_h[h][last:last + 1, :])
            dec_col = jnp.sum(jnp.where(eye_k, jnp.broadcast_to(dec_row, (DK, DK)), 0.0),
                              axis=1, keepdims=True)
            s_ref[g, h] = dec_col * sin[g, h] + _dot_tn(seg_mask(kt, g).astype(BF16), vg[h])
        next_filler()
    assert not fillers


def _front_kernel(*refs, L, seg, pipelined, carry, aliased):
    lead = 5 if pipelined else 2
    pos = lead
    c0_ref, n0_ref, m0_ref, s0_ref = refs[pos:pos + 4]
    pos += 4
    params = refs[pos:pos + 5]
    pos += 5 + (4 if aliased else 0)
    outs = refs[pos:pos + 7]
    pos += 7
    scratch = refs[pos:]
    c_ref, n_ref, m_ref, s_ref = outs[3:]

    if not pipelined:
        z_ref, zs_ref = refs[:lead]
        state_in = (c0_ref, n0_ref, m0_ref, s0_ref)
        if carry:
            @pl.when(pl.program_id(1) == 0)
            def _():
                c_ref[...] = c0_ref[...]
                n_ref[...] = n0_ref[...]
                m_ref[...] = m0_ref[...]
                s_ref[...] = s0_ref[...]
            state_in = (c_ref, n_ref, m_ref, s_ref)
        _mix_chunk(z_ref, zs_ref, 0, state_in, params, outs, L=L, seg=seg)
        return

    h_cur, h_nxt, g_ref, wbig_ref, wsm_ref = refs[:lead]
    stages = functools.partial(_project_stages, g_ref=g_ref, wbig_ref=wbig_ref, wsm_ref=wsm_ref)
    z_a, zs_a, z_b, zs_b = scratch

    @pl.when(pl.program_id(1) == 0)
    def _():
        c_ref[...] = c0_ref[...]
        n_ref[...] = n0_ref[...]
        m_ref[...] = m0_ref[...]
        s_ref[...] = s0_ref[...]

    @pl.when((pl.program_id(0) == 0) & (pl.program_id(1) == 0))
    def _():
        for stage in stages(h_cur, 0, L, z_ref=z_a, zs_ref=zs_a):
            stage()

    carried = (c_ref, n_ref, m_ref, s_ref)
    _mix_chunk(z_a, zs_a, 0, carried, params, outs, L=L, seg=seg,
               make_fillers=lambda: stages(h_cur, L, L, z_ref=z_b, zs_ref=zs_b))
    _mix_chunk(z_b, zs_b, L, carried, params, outs, L=L, seg=seg,
               make_fillers=lambda: stages(h_nxt, 0, L, z_ref=z_a, zs_ref=zs_a))


def _front(h, states_in, in_layer, states_out, layer, W, *, L, seg, seq_rows):
    T = h.shape[0]
    nseg = L // seg
    B = states_in[0].shape[1]
    pipelined = FUSE_PROJECTION and seq_rows > L
    rows = 2 * L if pipelined else L
    steps = max(seq_rows // rows, 1)
    grid = (B // nseg, steps)
    row_map = lambda b, k: (b * steps + k, 0)
    last_step = (B // nseg) * steps - 1
    nxt_map = lambda b, k: (2 * jnp.minimum(b * steps + k + 1, last_step), 0)
    par = lambda b, k: (layer, 0, 0)

    def state_specs(at):
        return [
            pl.BlockSpec((None, nseg, HEADS, DK, DV), lambda b, k: (at, b, 0, 0, 0)),
            pl.BlockSpec((None, nseg, HEADS, DK), lambda b, k: (at, b, 0, 0)),
            pl.BlockSpec((None, nseg, 1, HEADS), lambda b, k: (at, b, 0, 0)),
            pl.BlockSpec((None, nseg, HEADS, DK, DV), lambda b, k: (at, b, 0, 0, 0)),
        ]

    if pipelined:
        in_specs = [
            pl.BlockSpec((rows, D_MODEL), row_map),
            pl.BlockSpec((L, D_MODEL), nxt_map),
            _resident((None, 1, D_MODEL), par),
            _resident((None, D_MODEL, Z_BIG), par),
            _resident((None, D_MODEL, Z_SMALL), par),
        ]
        args = [h, h, W["norm_mix_g"], W["w_big"], W["w_small"]]
    else:
        in_specs = [pl.BlockSpec((rows, Z_BIG), row_map), pl.BlockSpec((rows, Z_SMALL), row_map)]
        args = list(_in_proj(h, layer, W))
    in_specs += [
        *state_specs(in_layer),
        _resident((None, 1, Z_SMALL), par),
        _resident((None, Z_SMALL, HEADS * DK), par),
        _resident((None, 1, HEADS * DK), par),
        _resident((None, 1, HEADS * DV), par),
        _resident((None, 1, HEADS * DV), par),
    ]
    args += [*states_in, W["gate_bias"], W["w_lr2"], W["b_lr2"], W["g_mnorm"], W["g_gnorm"]]
    aliases = {}
    if states_out is not None:
        for n, s in enumerate(states_out):
            aliases[len(args)] = 3 + n
            in_specs.append(pl.BlockSpec(memory_space=pl.ANY))
            args.append(s)
    out_shape = [
        jax.ShapeDtypeStruct((T, HEADS * DV), BF16),
        jax.ShapeDtypeStruct((T, HEADS * DV), BF16),
        jax.ShapeDtypeStruct((T, Z_GATES), BF16),
        jax.ShapeDtypeStruct((DEPTH, B, HEADS, DK, DV), F32),
        jax.ShapeDtypeStruct((DEPTH, B, HEADS, DK), F32),
        jax.ShapeDtypeStruct((DEPTH, B, 1, HEADS), F32),
        jax.ShapeDtypeStruct((DEPTH, B, HEADS, DK, DV), F32),
    ]
    z_scratch = [pltpu.VMEM((L, Z_BIG), F32), pltpu.VMEM((L, Z_SMALL), F32)]
    outs = pl.pallas_call(
        functools.partial(_front_kernel, L=L, seg=seg, pipelined=pipelined, carry=steps > 1,
                          aliased=states_out is not None),
        grid=grid,
        in_specs=in_specs,
        out_specs=[
            pl.BlockSpec((rows, HEADS * DV), row_map),
            pl.BlockSpec((rows, HEADS * DV), row_map),
            pl.BlockSpec((rows, Z_GATES), row_map),
            *state_specs(layer),
        ],
        out_shape=out_shape,
        input_output_aliases=aliases,
        scratch_shapes=z_scratch * 2 if pipelined else [],
        compiler_params=pltpu.CompilerParams(
            dimension_semantics=("arbitrary", "arbitrary"),
            vmem_limit_bytes=VMEM_LIMIT_BYTES,
        ),
        name="front",
    )(*args)
    return outs[0], outs[1], outs[2], tuple(outs[3:])


def _channel_kernel(*refs, seq_tiles, has_state, final):
    (h_ref, hm_ref, hg_ref, ga_ref, gb_ref, wpa_ref, wpb_ref, wo_ref, gffn_ref,
     wup_ref, cw_ref, cb_ref, wd_ref, gple_ref, wpg_ref, wpp_ref, p_ref) = refs[:17]
    pos = 17
    pst_ref = gfin_ref = carry_scr = None
    if has_state:
        pst_ref = refs[pos]
        pos += 1
    if final:
        gfin_ref = refs[pos]
        pos += 1
    out_ref, conv_ref = refs[pos:pos + 2]
    if not has_state:
        carry_scr = refs[pos + 2]

    i = pl.program_id(0)
    tm = h_ref.shape[0]
    hr = tm // CHAN_SPLIT
    groups = [slice(r * hr, (r + 1) * hr) for r in range(CHAN_SPLIT)]

    pa = [_dot(hm_ref[rs, :], wpa_ref[...]) for rs in groups]
    pb = [_dot(hg_ref[rs, :], wpb_ref[...]) for rs in groups]
    h1, c = [], []
    for r, rs in enumerate(groups):
        merged = (_sigmoid(ga_ref[rs, :].astype(F32)) * pa[r]
                  + _sigmoid(gb_ref[rs, :].astype(F32)) * pb[r])
        h1.append(h_ref[rs, :] + _dot(merged.astype(BF16), wo_ref[...]))
        c.append(_rmsnorm(h1[r], gffn_ref[...]).astype(BF16))

    up = [[None] * CHAN_SPLIT for _ in range(2 * D_FF // UP_CHUNK)]
    for n in range(2 * D_FF // UP_CHUNK):
        for r in range(CHAN_SPLIT):
            up[n][r] = _dot(c[r], wup_ref[:, n * UP_CHUNK:(n + 1) * UP_CHUNK])

    def up_cols(start, r):
        n, off = divmod(start, UP_CHUNK)
        assert off + FF_CHUNK <= UP_CHUNK
        return up[n][r][:, off:off + FF_CHUNK]

    rowi = lax.broadcasted_iota(jnp.int32, (hr, FF_CHUNK), 0)
    acc = [None] * CHAN_SPLIT
    for n in range(D_FF // FF_CHUNK):
        ff = slice(n * FF_CHUNK, (n + 1) * FF_CHUNK)
        if not has_state:
            prev = jnp.where(i % seq_tiles == 0, 0.0, carry_scr[:, ff])
        for r, rs in enumerate(groups):
            ug = up_cols(n * FF_CHUNK, r)
            uv = up_cols(D_FF + n * FF_CHUNK, r)
            r1 = pltpu.roll(ug, 1, 0)
            r2 = pltpu.roll(ug, 2, 0)
            if has_state:
                pst = pst_ref[rs, ff]
                t = rowi & (SUBLANE - 1)
                u1 = jnp.where(t == 0, pltpu.roll(pst, hr - 1, 0), r1)
                u2 = jnp.where(t < 2, pst, r2)
                conv_ref[rs, ff] = ug
            else:
                p6 = jnp.broadcast_to(prev[6:7, :], ug.shape)
                p7 = jnp.broadcast_to(prev[7:8, :], ug.shape)
                u1 = jnp.where(rowi == 0, p7, r1)
                u2 = jnp.where(rowi == 0, p6, jnp.where(rowi == 1, p7, r2))
                prev = ug[hr - SUBLANE:, :]
            conv = cb_ref[:, ff] + cw_ref[0:1, ff] * u2 + cw_ref[1:2, ff] * u1 + cw_ref[2:3, ff] * ug
            act = conv * _sigmoid(conv) * uv
            part = _dot(act.astype(BF16), wd_ref[ff, :])
            acc[r] = part if acc[r] is None else acc[r] + part
        if not has_state:
            carry_scr[:, ff] = prev
            conv_ref[:, ff] = prev

    for r, rs in enumerate(groups):
        h2 = h1[r] + acc[r]
        e = _rmsnorm(h2, gple_ref[...]).astype(BF16)
        h3 = h2 + _sigmoid(_dot(e, wpg_ref[...])) * _dot(p_ref[rs, :].astype(BF16), wpp_ref[...])
        if final:
            h3 = _rmsnorm(h3, gfin_ref[...])
        out_ref[rs, :] = h3


def _channel(h, hm, hg, zg, p, pst, layer, W, *, seq_rows, final):
    T = h.shape[0]
    tm = CHAN_ROWS
    has_state = pst is not None
    seq_tiles = max(seq_rows // tm, 1)
    rows = lambda i: (i, 0)
    par = lambda i: (layer, 0, 0)
    in_specs = [
        pl.BlockSpec((tm, D_MODEL), rows),
        pl.BlockSpec((tm, D_MODEL), rows),
        pl.BlockSpec((tm, D_MODEL), rows),
        pl.BlockSpec((tm, D_MODEL), lambda i: (i, (GA - Z_MIX) // D_MODEL)),
        pl.BlockSpec((tm, D_MODEL), lambda i: (i, (GB - Z_MIX) // D_MODEL)),
        _resident((None, D_MODEL, D_MODEL), par),
        _resident((None, D_MODEL, D_MODEL), par),
        _resident((None, D_MODEL, D_MODEL), par),
        _resident((None, 1, D_MODEL), par),
        _resident((None, D_MODEL, 2 * D_FF), par),
        _resident((None, CONV_W, D_FF), par),
        _resident((None, 1, D_FF), par),
        _resident((None, D_FF, D_MODEL), par),
        _resident((None, 1, D_MODEL), par),
        _resident((None, D_MODEL, D_MODEL), par),
        _resident((None, PLE_DIM, D_MODEL), par),
        pl.BlockSpec((None, tm, PLE_DIM), lambda i: (layer, i, 0)),
    ]
    args = [h, hm, hg, zg, zg, W["w_pa"], W["w_pb"], W["w_o"], W["norm_ffn_g"],
            W["w_up"], W["conv_w"], W["conv_b"], W["w_down"], W["norm_ple_g"],
            W["w_ple_gate"], W["w_ple_proj"], p]
    scratch = []
    if has_state:
        in_specs.append(pl.BlockSpec((tm, D_FF), rows))
        args.append(pst)
        conv_spec = pl.BlockSpec((tm, D_FF), rows)
        conv_shape = jax.ShapeDtypeStruct((T, D_FF), F32)
    else:
        conv_spec = pl.BlockSpec((None, SUBLANE, D_FF), lambda i: (i, 0, 0))
        conv_shape = jax.ShapeDtypeStruct((T // tm, SUBLANE, D_FF), F32)
        scratch.append(pltpu.VMEM((SUBLANE, D_FF), F32))
    if final:
        in_specs.append(_resident((1, D_MODEL), lambda i: (0, 0)))
        args.append(W["final_norm_g"])
    return pl.pallas_call(
        functools.partial(_channel_kernel, seq_tiles=seq_tiles, has_state=has_state, final=final),
        grid=(T // tm,),
        in_specs=in_specs,
        out_specs=[pl.BlockSpec((tm, D_MODEL), rows), conv_spec],
        out_shape=[jax.ShapeDtypeStruct((T, D_MODEL), F32), conv_shape],
        scratch_shapes=scratch,
        compiler_params=pltpu.CompilerParams(
            dimension_semantics=("arbitrary",),
            vmem_limit_bytes=VMEM_LIMIT_BYTES,
        ),
        name="channel",
    )(*args)


def _prep_weights(norm_mix_g, w_in, b_mi, b_mf, g_mnorm, w_lr2, b_lr2, g_gnorm, w_pa, w_pb, w_o,
                  norm_ffn_g, w_up, conv_w, conv_b, w_down, norm_ple_g, w_ple_gate, w_ple_proj,
                  final_norm_g):
    gate_bias = jnp.concatenate([b_mi, b_mf, jnp.zeros((DEPTH, Z_SMALL - 2 * HEADS), F32)], axis=-1)
    w_lr2_pad = jnp.zeros((DEPTH, Z_SMALL, HEADS * DK), F32).at[:, LR_LANE:LR_LANE + G_RANK, :].set(w_lr2)
    row = lambda x: x.reshape(DEPTH, 1, x.shape[-1])
    w_in_t = jnp.swapaxes(w_in, 1, 2)
    return dict(
        norm_mix_g=row(norm_mix_g), w_big=_repack_w_in(w_in_t), w_small=_narrow_w_in(w_in_t),
        gate_bias=row(gate_bias), w_lr2=w_lr2_pad.astype(BF16), b_lr2=row(b_lr2),
        g_mnorm=row(g_mnorm), g_gnorm=row(g_gnorm),
        w_pa=w_pa.astype(BF16), w_pb=w_pb.astype(BF16), w_o=w_o.astype(BF16),
        norm_ffn_g=row(norm_ffn_g), w_up=w_up.astype(BF16), conv_w=conv_w, conv_b=row(conv_b),
        w_down=w_down.astype(BF16), norm_ple_g=row(norm_ple_g),
        w_ple_gate=w_ple_gate.astype(BF16), w_ple_proj=w_ple_proj.astype(BF16),
        final_norm_g=final_norm_g.reshape(1, D_MODEL),
    )


def _trunk(x, p, states, stacked, cv0, W, *, seq_rows):
    T = x.shape[0]
    B = T // seq_rows
    if seq_rows >= 2 * PROMPT_CHUNK:
        L, seg = PROMPT_CHUNK, PROMPT_CHUNK
    else:
        L, seg = SAMPLE_GROUP * seq_rows, seq_rows
    h = x
    new_states = None
    cvs = []
    for layer in range(DEPTH):
        hm, hg, zg, new_states = _front(h, states, layer if stacked else 0, new_states, layer, W,
                                        L=L, seg=seg, seq_rows=seq_rows)
        if cv0 is None:
            pst = None
        else:
            pst = jnp.pad(cv0[layer], ((0, 0), (0, seq_rows - (CONV_W - 1)), (0, 0))).reshape(T, D_FF)
        h, cv = _channel(h, hm, hg, zg, p, pst, layer, W, seq_rows=seq_rows, final=layer == DEPTH - 1)
        if cv0 is None:
            cv = cv.reshape(B, -1, SUBLANE, D_FF)[:, -1, SUBLANE - (CONV_W - 1):, :]
        else:
            cv = cv.reshape(B, seq_rows, D_FF)[:, seq_rows - (CONV_W - 1):, :]
        cvs.append(cv)
    c, n, m, s = new_states
    return h, c, n, m.reshape(DEPTH, B, HEADS), s, jnp.stack(cvs)


def kernel(x_prompt, x_sample, state_mlstm_C, state_mlstm_n, state_mlstm_m, state_gla_S, state_ffn_conv, p_prompt, p_sample, norm_mix_g, w_in, b_mi, b_mf, g_mnorm, w_lr2, b_lr2, g_gnorm, w_pa, w_pb, w_o, norm_ffn_g, w_up, conv_w, conv_b, w_down, norm_ple_g, w_ple_gate, w_ple_proj, final_norm_g):
    W = _prep_weights(norm_mix_g, w_in, b_mi, b_mf, g_mnorm, w_lr2, b_lr2, g_gnorm, w_pa, w_pb, w_o,
                      norm_ffn_g, w_up, conv_w, conv_b, w_down, norm_ple_g, w_ple_gate, w_ple_proj,
                      final_norm_g)
    bp, sp, _ = x_prompt.shape
    bs, ss, _ = x_sample.shape
    zero_states = (jnp.zeros((1, bp, HEADS, DK, DV), F32), jnp.zeros((1, bp, HEADS, DK), F32),
                   jnp.zeros((1, bp, 1, HEADS), F32), jnp.zeros((1, bp, HEADS, DK, DV), F32))
    yp, c_p, n_p, m_p, s_p, cv_p = _trunk(
        x_prompt.reshape(bp * sp, D_MODEL), p_prompt.reshape(DEPTH, bp * sp, PLE_DIM),
        zero_states, False, None, W, seq_rows=sp)
    sample_states = (state_mlstm_C, state_mlstm_n, state_mlstm_m.reshape(DEPTH, bs, 1, HEADS), state_gla_S)
    ys, c_s, n_s, m_s, s_s, cv_s = _trunk(
        x_sample.reshape(bs * ss, D_MODEL), p_sample.reshape(DEPTH, bs * ss, PLE_DIM),
        sample_states, True, state_ffn_conv, W, seq_rows=ss)
    return (yp.reshape(bp, sp, D_MODEL), ys.reshape(bs, ss, D_MODEL),
            c_p, n_p, m_p, s_p, cv_p, c_s, n_s, m_s, s_s, cv_s)
```

```python
import functools

import jax
import jax.numpy as jnp
from jax import lax
from jax.experimental import pallas as pl
from jax.experimental.pallas import tpu as pltpu

F32 = jnp.float32
BF16 = jnp.bfloat16
HIGHEST = lax.Precision.HIGHEST

D_MODEL = 1024
DEPTH = 4
HEADS = 4
DK = 128
DV = 256
G_RANK = 16
G_TAU = 16.0
D_FF = 2688
CONV_W = 3
PLE_DIM = 256
EPS = 1e-6
SCALE_K = DK ** -0.5

MQ, MK, MV, MO = 0, 512, 1024, 2048
GQ, GK, GV, GG = 3072, 3584, 4096, 5120
GA, GB = 6144, 7168
Z_BIG = 8192
Z_MIX = 6144
Z_GATES = Z_BIG - Z_MIX
Z_SMALL = 128
LR_LANE = 8
SRC_MI, SRC_GLR = 2048, 5128
SHIFT_AFTER_MI = 2 * HEADS
SHIFT_AFTER_GLR = 2 * HEADS + G_RANK

LANE = 128
SUBLANE = 8
VMEM_LIMIT_BYTES = 60 * 1024 * 1024

PROJ_ROWS = 512
PROJ_COLS = 256
CHAN_ROWS = 256
CHAN_SPLIT = 2
UP_CHUNK = 1792
FF_CHUNK = 896
REPACK_COLS = 1024
PROMPT_CHUNK = 256
SAMPLE_GROUP = 4
GLA_SAFE_LOG_DECAY = -60.0


def _dot(a, b):
    return jnp.dot(a, b, preferred_element_type=F32)


def _dot_nt(a, b):
    return lax.dot_general(a, b, (((1,), (1,)), ((), ())), preferred_element_type=F32)


def _dot_tn(a, b):
    return lax.dot_general(a, b, (((0,), (0,)), ((), ())), preferred_element_type=F32)


def _log_sigmoid(x):
    return jnp.minimum(x, 0.0) - jnp.log(1.0 + jnp.exp(-jnp.abs(x)))


def _masked_cumsum(mask_bf16, x, pieces):
    acc = None
    rest = x
    for n in range(pieces):
        piece = rest.astype(BF16)
        part = _dot(mask_bf16, piece)
        acc = part if acc is None else acc + part
        if n + 1 < pieces:
            rest = rest - piece.astype(F32)
    return acc


def _sigmoid(x):
    return 0.5 * jnp.tanh(0.5 * x) + 0.5


def _rmsnorm(x, g):
    return x * lax.rsqrt(jnp.mean(x * x, axis=-1, keepdims=True) + EPS) * g


def _resident(block_shape, index_map):
    return pl.BlockSpec(block_shape, index_map, pipeline_mode=pl.Buffered(1))


REPACK_TAIL = 32


def _repack_kernel(a_ref, b_ref, o_ref):
    t = pl.program_id(1)
    first_after_mi = MO // REPACK_COLS
    first_after_glr = GG // REPACK_COLS

    def emit(shift):
        if shift == 0:
            y = a_ref[...]
        else:
            x = jnp.concatenate([a_ref[...], b_ref[...]], axis=0)
            y = x[shift:shift + REPACK_COLS, :]
        o_ref[...] = y.T.astype(BF16)

    @pl.when(t < first_after_mi)
    def _():
        emit(0)

    @pl.when((t >= first_after_mi) & (t < first_after_glr))
    def _():
        emit(SHIFT_AFTER_MI)

    @pl.when(t >= first_after_glr)
    def _():
        emit(SHIFT_AFTER_GLR)


def _repack_w_in(w_in_t):
    tails_per_tile = REPACK_COLS // REPACK_TAIL
    return pl.pallas_call(
        _repack_kernel,
        grid=(DEPTH, Z_BIG // REPACK_COLS),
        in_specs=[
            pl.BlockSpec((None, REPACK_COLS, D_MODEL), lambda l, t: (l, t, 0)),
            pl.BlockSpec((None, REPACK_TAIL, D_MODEL), lambda l, t: (l, tails_per_tile * (t + 1), 0)),
        ],
        out_specs=pl.BlockSpec((None, D_MODEL, REPACK_COLS), lambda l, t: (l, 0, t)),
        out_shape=jax.ShapeDtypeStruct((DEPTH, D_MODEL, Z_BIG), BF16),
        compiler_params=pltpu.CompilerParams(
            dimension_semantics=("arbitrary", "arbitrary"),
            vmem_limit_bytes=VMEM_LIMIT_BYTES,
        ),
        name="repack_w_in",
    )(w_in_t, w_in_t)


def _narrow_kernel(mi_ref, glr0_ref, glr1_ref, o_ref):
    pad = jnp.zeros((Z_SMALL - SHIFT_AFTER_GLR, D_MODEL), F32)
    x = jnp.concatenate([mi_ref[...], glr0_ref[...], glr1_ref[...], pad], axis=0)
    o_ref[...] = x.T.astype(BF16)


def _narrow_w_in(w_in_t):
    assert SRC_MI % SUBLANE == 0 and SRC_GLR % SUBLANE == 0 and G_RANK == 2 * SUBLANE
    rows8 = lambda r: pl.BlockSpec((None, SUBLANE, D_MODEL), lambda l: (l, r // SUBLANE, 0))
    return pl.pallas_call(
        _narrow_kernel,
        grid=(DEPTH,),
        in_specs=[rows8(SRC_MI), rows8(SRC_GLR), rows8(SRC_GLR + SUBLANE)],
        out_specs=pl.BlockSpec((None, D_MODEL, Z_SMALL), lambda l: (l, 0, 0)),
        out_shape=jax.ShapeDtypeStruct((DEPTH, D_MODEL, Z_SMALL), BF16),
        compiler_params=pltpu.CompilerParams(dimension_semantics=("arbitrary",)),
        name="narrow_w_in",
    )(w_in_t, w_in_t, w_in_t)


def _in_proj_kernel(h_ref, g_ref, wbig_ref, wsm_ref, zbig_ref, zsm_ref):
    a = _rmsnorm(h_ref[...], g_ref[...]).astype(BF16)
    zsm_ref[...] = _dot(a, wsm_ref[...])
    for n in range(Z_BIG // PROJ_COLS):
        cols = slice(n * PROJ_COLS, (n + 1) * PROJ_COLS)
        zbig_ref[:, cols] = _dot(a, wbig_ref[:, cols])


def _in_proj(h, layer, W):
    T = h.shape[0]
    par = lambda i: (layer, 0, 0)
    return pl.pallas_call(
        _in_proj_kernel,
        grid=(T // PROJ_ROWS,),
        in_specs=[
            pl.BlockSpec((PROJ_ROWS, D_MODEL), lambda i: (i, 0)),
            _resident((None, 1, D_MODEL), par),
            _resident((None, D_MODEL, Z_BIG), par),
            _resident((None, D_MODEL, Z_SMALL), par),
        ],
        out_specs=[
            pl.BlockSpec((PROJ_ROWS, Z_BIG), lambda i: (i, 0)),
            pl.BlockSpec((PROJ_ROWS, Z_SMALL), lambda i: (i, 0)),
        ],
        out_shape=[
            jax.ShapeDtypeStruct((T, Z_BIG), F32),
            jax.ShapeDtypeStruct((T, Z_SMALL), F32),
        ],
        compiler_params=pltpu.CompilerParams(
            dimension_semantics=("arbitrary",),
            vmem_limit_bytes=VMEM_LIMIT_BYTES,
        ),
        name="in_proj",
    )(h, W["norm_mix_g"], W["w_big"], W["w_small"])


MIX_SLOTS = 1 + 8 * HEADS


def _project_stages(h_ref, r0, rows, g_ref, wbig_ref, wsm_ref, z_ref, zs_ref):
    held = {}

    def head():
        a = _rmsnorm(h_ref[r0:r0 + rows, :], g_ref[...]).astype(BF16)
        held["a"] = a
        zs_ref[...] = _dot(a, wsm_ref[...])

    def body(n):
        cols = slice(n * PROJ_COLS, (n + 1) * PROJ_COLS)
        z_ref[:, cols] = _dot(held["a"], wbig_ref[:, cols])

    return [head] + [functools.partial(body, n) for n in range(Z_BIG // PROJ_COLS)]


def _anchor(x, half):
    rows, width = x.shape
    if half >= SUBLANE:
        parts = []
        for blk in range(rows // (2 * half)):
            r = blk * 2 * half + half - 1
            parts.append(jnp.broadcast_to(x[r:r + 1, :], (2 * half, width)))
        return parts[0] if len(parts) == 1 else jnp.concatenate(parts, axis=0)
    x3 = x.reshape(rows // SUBLANE, SUBLANE, width)
    sub = lax.broadcasted_iota(jnp.int32, x3.shape, 1)

    def bc(i):
        return jnp.broadcast_to(x3[:, i:i + 1, :], x3.shape)

    if half == 4:
        out = bc(3)
    elif half == 2:
        out = jnp.where(sub < 4, bc(1), bc(5))
    else:
        out = jnp.where(sub < 2, bc(0), jnp.where(sub < 4, bc(2), jnp.where(sub < 6, bc(4), bc(6))))
    return out.reshape(rows, width)


def _causal_mask(L, seg):
    row = lax.broadcasted_iota(jnp.int32, (L, L), 0)
    col = lax.broadcasted_iota(jnp.int32, (L, L), 1)
    return (col <= row) & ((row ^ col) < seg)


def _mix_chunk(zb_ref, zs_ref, r0, state_in, params, outs, *, L, seg, make_fillers=None):
    w2_ref, blr_ref = params[1:3]
    tril_b = _causal_mask(L, seg).astype(F32).astype(BF16)
    la = _log_sigmoid(_dot(zs_ref[...].astype(BF16), w2_ref[...]) + blr_ref[...]) * (1.0 / G_TAU)
    bc = _masked_cumsum(tril_b, la, 2)

    def run(single_anchor):
        fillers = make_fillers() if make_fillers else ()
        _mix_rest(zb_ref, zs_ref, bc, r0, state_in, params, outs, L=L, seg=seg,
                  single_anchor=single_anchor, fillers=fillers)

    lax.cond(jnp.min(bc) >= GLA_SAFE_LOG_DECAY, lambda: run(True), lambda: run(False))


def _mix_rest(zb_ref, zs_ref, bc, r0, state_in, params, outs, *, L, seg, single_anchor, fillers):
    cin, nin, min_, sin = state_in
    bias_ref, w2_ref, blr_ref, gm_ref, gg_ref = params
    hm_ref, hg_ref, zg_ref, c_ref, n_ref, m_ref, s_ref = outs
    orow = slice(r0, r0 + L)
    nseg = L // seg
    seg_shift = seg.bit_length() - 1

    row = lax.broadcasted_iota(jnp.int32, (L, L), 0)
    col = lax.broadcasted_iota(jnp.int32, (L, L), 1)
    rxc = row ^ col
    causal = (col <= row) & (rxc < seg)
    tril_b = causal.astype(F32).astype(BF16)
    lane = lax.broadcasted_iota(jnp.int32, (L, LANE), 1)
    rowl = lax.broadcasted_iota(jnp.int32, (L, LANE), 0)
    eye_k = (lax.broadcasted_iota(jnp.int32, (DK, DK), 0)
             == lax.broadcasted_iota(jnp.int32, (DK, DK), 1))
    lane4 = lax.broadcasted_iota(jnp.int32, (1, HEADS), 1)

    def seg_last(x):
        parts = [jnp.broadcast_to(x[(g + 1) * seg - 1:(g + 1) * seg, :], (seg, x.shape[1]))
                 for g in range(nseg)]
        return parts[0] if nseg == 1 else jnp.concatenate(parts, axis=0)

    def seg_rows(vals):
        parts = [jnp.broadcast_to(v, (seg, v.shape[1])) for v in vals]
        return parts[0] if nseg == 1 else jnp.concatenate(parts, axis=0)

    def seg_mask(x, g):
        return x if nseg == 1 else jnp.where((rowl >> seg_shift) == g, x, 0.0)

    def per_seg_dot(x_f32, state_ref, h):
        parts = [_dot(x_f32[g * seg:(g + 1) * seg, :].astype(BF16), state_ref[g, h].astype(BF16))
                 for g in range(nseg)]
        return parts[0] if nseg == 1 else jnp.concatenate(parts, axis=0)

    fillers = list(fillers)

    per_slot = -(-len(fillers) // MIX_SLOTS)

    def next_filler():
        for _ in range(min(per_slot, len(fillers))):
            fillers.pop(0)()

    next_filler()
    zg_ref[orow, :] = zb_ref[:, Z_MIX:].astype(BF16)

    zs = zs_ref[...]
    gates = zs + bias_ref[...]
    lf = _log_sigmoid(gates)
    cum = _masked_cumsum(tril_b, lf, 3)
    cols = jnp.where(lane < HEADS, gates, cum)
    sel = (lax.broadcasted_iota(jnp.int32, (SUBLANE, LANE), 0)
           == lax.broadcasted_iota(jnp.int32, (SUBLANE, LANE), 1)).astype(F32)
    rows_ = lax.dot_general(sel, cols, (((1,), (1,)), ((), ())),
                            precision=HIGHEST, preferred_element_type=F32)

    def lhs_with_state(a_intra, x_inter):
        return jnp.concatenate([a_intra.astype(BF16), x_inter.astype(BF16)], axis=1)

    def rhs_with_state(v, state_ref, h):
        return jnp.concatenate([v, state_ref[0, h].astype(BF16)], axis=0)

    m_old = [min_[g] for g in range(nseg)]
    m_new = [jnp.zeros((1, HEADS), F32) for _ in range(nseg)]

    heads = range(HEADS)

    qf = [zb_ref[:, MQ + DK * h:MQ + DK * (h + 1)] for h in heads]
    kf = [zb_ref[:, MK + DK * h:MK + DK * (h + 1)] for h in heads]
    vm = [zb_ref[:, MV + DV * h:MV + DV * (h + 1)].astype(BF16) for h in heads]
    li_c = [cols[:, h:h + 1] for h in heads]
    b_c = [cols[:, HEADS + h:HEADS + h + 1] for h in heads]
    mt, w_inter, amat_m = [], [], []
    for h in heads:
        li_r = rows_[h:h + 1, :]
        b_r = rows_[HEADS + h:HEADS + h + 1, :]
        m_col = seg_rows([m_old[g][:, h:h + 1] for g in range(nseg)])
        dmat = jnp.where(causal, b_c[h] + (li_r - b_r), -jnp.inf)
        inter = b_c[h] + m_col
        mt.append(jnp.maximum(inter, jnp.max(dmat, axis=1, keepdims=True)))
        w_inter.append(jnp.exp(inter - mt[h]))
        scores = _dot_nt(qf[h].astype(BF16), kf[h].astype(BF16))
        amat_m.append(jnp.exp(dmat - mt[h]) * (scores * SCALE_K))
        next_filler()

    qs = [zb_ref[:, GQ + DK * h:GQ + DK * (h + 1)] for h in heads]
    ks = [zb_ref[:, GK + DK * h:GK + DK * (h + 1)] for h in heads]
    vg = [zb_ref[:, GV + DV * h:GV + DV * (h + 1)].astype(BF16) for h in heads]
    b_h = [bc[:, DK * h:DK * (h + 1)] for h in heads]
    qe = [qs[h] * jnp.exp(b_h[h]) for h in heads]

    def intra_single_anchor():
        out = []
        for h in heads:
            ke = ks[h] * jnp.exp(-b_h[h])
            out.append(jnp.where(causal, _dot_nt(qe[h].astype(BF16), ke.astype(BF16)), 0.0))
            next_filler()
        return out

    def intra_by_levels():
        out = [jnp.where(row == col, jnp.sum(qs[h] * ks[h], axis=1, keepdims=True), 0.0)
               for h in heads]
        half = 1
        while half < seg:
            wgt = jnp.exp(-jnp.abs(bc - _anchor(bc, half)))
            pair = (col < row) & (rxc >= half) & (rxc < 2 * half)
            for h in heads:
                w_h = wgt[:, DK * h:DK * (h + 1)]
                lvl = _dot_nt((qs[h] * w_h).astype(BF16), (ks[h] * w_h).astype(BF16))
                out[h] = jnp.where(pair, lvl, out[h])
            half *= 2
        for h in heads:
            next_filler()
        return out

    amat_g = intra_single_anchor() if single_anchor else intra_by_levels()

    hh = []
    for h in heads:
        n_rows = seg_rows([nin[g, h:h + 1, :] for g in range(nseg)])
        qn = jnp.sum(qf[h] * n_rows, axis=1, keepdims=True)
        if nseg == 1:
            num = _dot(lhs_with_state(amat_m[h], w_inter[h] * qf[h]), rhs_with_state(vm[h], cin, h))
        else:
            num = _dot(amat_m[h].astype(BF16), vm[h]) + w_inter[h] * per_seg_dot(qf[h], cin, h)
        den = jnp.sum(amat_m[h], axis=1, keepdims=True) + w_inter[h] * qn
        den = jnp.maximum(jnp.abs(den), jnp.exp(-mt[h]))
        hh.append(num * (1.0 / den))
        next_filler()

    og_ = []
    for h in heads:
        if nseg == 1:
            o = _dot(lhs_with_state(amat_g[h] * SCALE_K, qe[h] * SCALE_K), rhs_with_state(vg[h], sin, h))
        else:
            o = _dot((amat_g[h] * SCALE_K).astype(BF16), vg[h]) + SCALE_K * per_seg_dot(qe[h], sin, h)
        og_.append(o)
        next_filler()

    for h in heads:
        mu = jnp.mean(hh[h], axis=-1, keepdims=True)
        hc = hh[h] - mu
        hn = hc * lax.rsqrt(jnp.mean(hc * hc, axis=-1, keepdims=True) + EPS)
        og = zb_ref[:, MO + DV * h:MO + DV * (h + 1)]
        hm_ref[orow, DV * h:DV * (h + 1)] = (hn * gm_ref[:, DV * h:DV * (h + 1)] * _sigmoid(og)).astype(BF16)
        next_filler()

    for h in heads:
        o = og_[h]
        on = o * lax.rsqrt(jnp.mean(o * o, axis=-1, keepdims=True) + EPS)
        gg = zb_ref[:, GG + DV * h:GG + DV * (h + 1)]
        hg_ref[orow, DV * h:DV * (h + 1)] = (on * gg_ref[:, DV * h:DV * (h + 1)] * (gg * _sigmoid(gg))).astype(BF16)
        next_filler()

    for h in heads:
        ws = jnp.exp(seg_last(b_c[h]) - b_c[h] + li_c[h] - seg_last(mt[h])) * SCALE_K
        kw = kf[h] * ws
        for g in range(nseg):
            last = (g + 1) * seg - 1
            m_last = mt[h][last:last + 1, :]
            cw = jnp.exp(b_c[h][last:last + 1, :] + m_old[g][:, h:h + 1] - m_last)
            c_new = cw * cin[g, h] + _dot_tn(seg_mask(kw, g).astype(BF16), vm[h])
            n_new = cw * nin[g, h:h + 1, :] + jnp.sum(kw[g * seg:(g + 1) * seg, :], axis=0, keepdims=True)
            c_ref[g, h] = c_new
            n_ref[g, h:h + 1, :] = n_new
            m_new[g] = jnp.where(lane4 == h, m_last, m_new[g])
        next_filler()
    for g in range(nseg):
        m_ref[g] = m_new[g]

    bl_rows = seg_last(bc)
    for h in heads:
        kt = ks[h] * jnp.exp(bl_rows[:, DK * h:DK * (h + 1)] - b_h[h])
        for g in range(nseg):
            last = (g + 1) * seg - 1
            dec_row = jnp.exp(b_h[h][last:last + 1, :])
            dec_col = jnp.sum(jnp.where(eye_k, jnp.broadcast_to(dec_row, (DK, DK)), 0.0),
                              axis=1, keepdims=True)
            s_ref[g, h] = dec_col * sin[g, h] + _dot_tn(seg_mask(kt, g).astype(BF16), vg[h])
        next_filler()
    assert not fillers


def _front_kernel(*refs, L, seg, pipelined, aliased):
    lead = 5 if pipelined else 2
    pos = lead
    c0_ref, n0_ref, m0_ref, s0_ref = refs[pos:pos + 4]
    pos += 4
    params = refs[pos:pos + 5]
    pos += 5 + (4 if aliased else 0)
    outs = refs[pos:pos + 7]
    pos += 7
    scratch = refs[pos:]
    c_ref, n_ref, m_ref, s_ref = outs[3:]

    if not pipelined:
        z_ref, zs_ref = refs[:lead]
        _mix_chunk(z_ref, zs_ref, 0, (c0_ref, n0_ref, m0_ref, s0_ref), params, outs, L=L, seg=seg)
        return

    h_cur, h_nxt, g_ref, wbig_ref, wsm_ref = refs[:lead]
    stages = functools.partial(_project_stages, g_ref=g_ref, wbig_ref=wbig_ref, wsm_ref=wsm_ref)
    z_a, zs_a, z_b, zs_b = scratch

    @pl.when(pl.program_id(1) == 0)
    def _():
        c_ref[...] = c0_ref[...]
        n_ref[...] = n0_ref[...]
        m_ref[...] = m0_ref[...]
        s_ref[...] = s0_ref[...]

    @pl.when((pl.program_id(0) == 0) & (pl.program_id(1) == 0))
    def _():
        for stage in stages(h_cur, 0, L, z_ref=z_a, zs_ref=zs_a):
            stage()

    carried = (c_ref, n_ref, m_ref, s_ref)
    _mix_chunk(z_a, zs_a, 0, carried, params, outs, L=L, seg=seg,
               make_fillers=lambda: stages(h_cur, L, L, z_ref=z_b, zs_ref=zs_b))
    _mix_chunk(z_b, zs_b, L, carried, params, outs, L=L, seg=seg,
               make_fillers=lambda: stages(h_nxt, 0, L, z_ref=z_a, zs_ref=zs_a))


def _front(h, states_in, in_layer, states_out, layer, W, *, L, seg, seq_rows):
    T = h.shape[0]
    nseg = L // seg
    B = states_in[0].shape[1]
    pipelined = seq_rows > L
    rows = 2 * L if pipelined else L
    steps = seq_rows // rows if pipelined else 1
    grid = (B // nseg, steps)
    row_map = lambda b, k: (b * steps + k, 0)
    last_step = (B // nseg) * steps - 1
    nxt_map = lambda b, k: (2 * jnp.minimum(b * steps + k + 1, last_step), 0)
    par = lambda b, k: (layer, 0, 0)

    def state_specs(at):
        return [
            pl.BlockSpec((None, nseg, HEADS, DK, DV), lambda b, k: (at, b, 0, 0, 0)),
            pl.BlockSpec((None, nseg, HEADS, DK), lambda b, k: (at, b, 0, 0)),
            pl.BlockSpec((None, nseg, 1, HEADS), lambda b, k: (at, b, 0, 0)),
            pl.BlockSpec((None, nseg, HEADS, DK, DV), lambda b, k: (at, b, 0, 0, 0)),
        ]

    if pipelined:
        in_specs = [
            pl.BlockSpec((rows, D_MODEL), row_map),
            pl.BlockSpec((L, D_MODEL), nxt_map),
            _resident((None, 1, D_MODEL), par),
            _resident((None, D_MODEL, Z_BIG), par),
            _resident((None, D_MODEL, Z_SMALL), par),
        ]
        args = [h, h, W["norm_mix_g"], W["w_big"], W["w_small"]]
    else:
        in_specs = [pl.BlockSpec((rows, Z_BIG), row_map), pl.BlockSpec((rows, Z_SMALL), row_map)]
        args = list(_in_proj(h, layer, W))
    in_specs += [
        *state_specs(in_layer),
        _resident((None, 1, Z_SMALL), par),
        _resident((None, Z_SMALL, HEADS * DK), par),
        _resident((None, 1, HEADS * DK), par),
        _resident((None, 1, HEADS * DV), par),
        _resident((None, 1, HEADS * DV), par),
    ]
    args += [*states_in, W["gate_bias"], W["w_lr2"], W["b_lr2"], W["g_mnorm"], W["g_gnorm"]]
    aliases = {}
    if states_out is not None:
        for n, s in enumerate(states_out):
            aliases[len(args)] = 3 + n
            in_specs.append(pl.BlockSpec(memory_space=pl.ANY))
            args.append(s)
    out_shape = [
        jax.ShapeDtypeStruct((T, HEADS * DV), BF16),
        jax.ShapeDtypeStruct((T, HEADS * DV), BF16),
        jax.ShapeDtypeStruct((T, Z_GATES), BF16),
        jax.ShapeDtypeStruct((DEPTH, B, HEADS, DK, DV), F32),
        jax.ShapeDtypeStruct((DEPTH, B, HEADS, DK), F32),
        jax.ShapeDtypeStruct((DEPTH, B, 1, HEADS), F32),
        jax.ShapeDtypeStruct((DEPTH, B, HEADS, DK, DV), F32),
    ]
    z_scratch = [pltpu.VMEM((L, Z_BIG), F32), pltpu.VMEM((L, Z_SMALL), F32)]
    outs = pl.pallas_call(
        functools.partial(_front_kernel, L=L, seg=seg, pipelined=pipelined, aliased=states_out is not None),
        grid=grid,
        in_specs=in_specs,
        out_specs=[
            pl.BlockSpec((rows, HEADS * DV), row_map),
            pl.BlockSpec((rows, HEADS * DV), row_map),
            pl.BlockSpec((rows, Z_GATES), row_map),
            *state_specs(layer),
        ],
        out_shape=out_shape,
        input_output_aliases=aliases,
        scratch_shapes=z_scratch * 2 if pipelined else [],
        compiler_params=pltpu.CompilerParams(
            dimension_semantics=("arbitrary", "arbitrary"),
            vmem_limit_bytes=VMEM_LIMIT_BYTES,
        ),
        name="front",
    )(*args)
    return outs[0], outs[1], outs[2], tuple(outs[3:])


def _channel_kernel(*refs, seq_tiles, has_state, final):
    (h_ref, hm_ref, hg_ref, ga_ref, gb_ref, wpa_ref, wpb_ref, wo_ref, gffn_ref,
     wup_ref, cw_ref, cb_ref, wd_ref, gple_ref, wpg_ref, wpp_ref, p_ref) = refs[:17]
    pos = 17
    pst_ref = gfin_ref = carry_scr = None
    if has_state:
        pst_ref = refs[pos]
        pos += 1
    if final:
        gfin_ref = refs[pos]
        pos += 1
    out_ref, conv_ref = refs[pos:pos + 2]
    if not has_state:
        carry_scr = refs[pos + 2]

    i = pl.program_id(0)
    tm = h_ref.shape[0]
    hr = tm // CHAN_SPLIT
    groups = [slice(r * hr, (r + 1) * hr) for r in range(CHAN_SPLIT)]

    pa = [_dot(hm_ref[rs, :], wpa_ref[...]) for rs in groups]
    pb = [_dot(hg_ref[rs, :], wpb_ref[...]) for rs in groups]
    h1, c = [], []
    for r, rs in enumerate(groups):
        merged = (_sigmoid(ga_ref[rs, :].astype(F32)) * pa[r]
                  + _sigmoid(gb_ref[rs, :].astype(F32)) * pb[r])
        h1.append(h_ref[rs, :] + _dot(merged.astype(BF16), wo_ref[...]))
        c.append(_rmsnorm(h1[r], gffn_ref[...]).astype(BF16))

    up = [[None] * CHAN_SPLIT for _ in range(2 * D_FF // UP_CHUNK)]
    for n in range(2 * D_FF // UP_CHUNK):
        for r in range(CHAN_SPLIT):
            up[n][r] = _dot(c[r], wup_ref[:, n * UP_CHUNK:(n + 1) * UP_CHUNK])

    def up_cols(start, r):
        n, off = divmod(start, UP_CHUNK)
        assert off + FF_CHUNK <= UP_CHUNK
        return up[n][r][:, off:off + FF_CHUNK]

    rowi = lax.broadcasted_iota(jnp.int32, (hr, FF_CHUNK), 0)
    acc = [None] * CHAN_SPLIT
    for n in range(D_FF // FF_CHUNK):
        ff = slice(n * FF_CHUNK, (n + 1) * FF_CHUNK)
        if not has_state:
            prev = jnp.where(i % seq_tiles == 0, 0.0, carry_scr[:, ff])
        for r, rs in enumerate(groups):
            ug = up_cols(n * FF_CHUNK, r)
            uv = up_cols(D_FF + n * FF_CHUNK, r)
            r1 = pltpu.roll(ug, 1, 0)
            r2 = pltpu.roll(ug, 2, 0)
            if has_state:
                pst = pst_ref[rs, ff]
                t = rowi & (SUBLANE - 1)
                u1 = jnp.where(t == 0, pltpu.roll(pst, hr - 1, 0), r1)
                u2 = jnp.where(t < 2, pst, r2)
                conv_ref[rs, ff] = ug
            else:
                p6 = jnp.broadcast_to(prev[6:7, :], ug.shape)
                p7 = jnp.broadcast_to(prev[7:8, :], ug.shape)
                u1 = jnp.where(rowi == 0, p7, r1)
                u2 = jnp.where(rowi == 0, p6, jnp.where(rowi == 1, p7, r2))
                prev = ug[hr - SUBLANE:, :]
            conv = cb_ref[:, ff] + cw_ref[0:1, ff] * u2 + cw_ref[1:2, ff] * u1 + cw_ref[2:3, ff] * ug
            act = conv * _sigmoid(conv) * uv
            part = _dot(act.astype(BF16), wd_ref[ff, :])
            acc[r] = part if acc[r] is None else acc[r] + part
        if not has_state:
            carry_scr[:, ff] = prev
            conv_ref[:, ff] = prev

    for r, rs in enumerate(groups):
        h2 = h1[r] + acc[r]
        e = _rmsnorm(h2, gple_ref[...]).astype(BF16)
        h3 = h2 + _sigmoid(_dot(e, wpg_ref[...])) * _dot(p_ref[rs, :].astype(BF16), wpp_ref[...])
        if final:
            h3 = _rmsnorm(h3, gfin_ref[...])
        out_ref[rs, :] = h3


def _channel(h, hm, hg, zg, p, pst, layer, W, *, seq_rows, final):
    T = h.shape[0]
    tm = CHAN_ROWS
    has_state = pst is not None
    seq_tiles = max(seq_rows // tm, 1)
    rows = lambda i: (i, 0)
    par = lambda i: (layer, 0, 0)
    in_specs = [
        pl.BlockSpec((tm, D_MODEL), rows),
        pl.BlockSpec((tm, D_MODEL), rows),
        pl.BlockSpec((tm, D_MODEL), rows),
        pl.BlockSpec((tm, D_MODEL), lambda i: (i, (GA - Z_MIX) // D_MODEL)),
        pl.BlockSpec((tm, D_MODEL), lambda i: (i, (GB - Z_MIX) // D_MODEL)),
        _resident((None, D_MODEL, D_MODEL), par),
        _resident((None, D_MODEL, D_MODEL), par),
        _resident((None, D_MODEL, D_MODEL), par),
        _resident((None, 1, D_MODEL), par),
        _resident((None, D_MODEL, 2 * D_FF), par),
        _resident((None, CONV_W, D_FF), par),
        _resident((None, 1, D_FF), par),
        _resident((None, D_FF, D_MODEL), par),
        _resident((None, 1, D_MODEL), par),
        _resident((None, D_MODEL, D_MODEL), par),
        _resident((None, PLE_DIM, D_MODEL), par),
        pl.BlockSpec((None, tm, PLE_DIM), lambda i: (layer, i, 0)),
    ]
    args = [h, hm, hg, zg, zg, W["w_pa"], W["w_pb"], W["w_o"], W["norm_ffn_g"],
            W["w_up"], W["conv_w"], W["conv_b"], W["w_down"], W["norm_ple_g"],
            W["w_ple_gate"], W["w_ple_proj"], p]
    scratch = []
    if has_state:
        in_specs.append(pl.BlockSpec((tm, D_FF), rows))
        args.append(pst)
        conv_spec = pl.BlockSpec((tm, D_FF), rows)
        conv_shape = jax.ShapeDtypeStruct((T, D_FF), F32)
    else:
        conv_spec = pl.BlockSpec((None, SUBLANE, D_FF), lambda i: (i, 0, 0))
        conv_shape = jax.ShapeDtypeStruct((T // tm, SUBLANE, D_FF), F32)
        scratch.append(pltpu.VMEM((SUBLANE, D_FF), F32))
    if final:
        in_specs.append(_resident((1, D_MODEL), lambda i: (0, 0)))
        args.append(W["final_norm_g"])
    return pl.pallas_call(
        functools.partial(_channel_kernel, seq_tiles=seq_tiles, has_state=has_state, final=final),
        grid=(T // tm,),
        in_specs=in_specs,
        out_specs=[pl.BlockSpec((tm, D_MODEL), rows), conv_spec],
        out_shape=[jax.ShapeDtypeStruct((T, D_MODEL), F32), conv_shape],
        scratch_shapes=scratch,
        compiler_params=pltpu.CompilerParams(
            dimension_semantics=("arbitrary",),
            vmem_limit_bytes=VMEM_LIMIT_BYTES,
        ),
        name="channel",
    )(*args)


def _prep_weights(norm_mix_g, w_in, b_mi, b_mf, g_mnorm, w_lr2, b_lr2, g_gnorm, w_pa, w_pb, w_o,
                  norm_ffn_g, w_up, conv_w, conv_b, w_down, norm_ple_g, w_ple_gate, w_ple_proj,
                  final_norm_g):
    gate_bias = jnp.concatenate([b_mi, b_mf, jnp.zeros((DEPTH, Z_SMALL - 2 * HEADS), F32)], axis=-1)
    w_lr2_pad = jnp.zeros((DEPTH, Z_SMALL, HEADS * DK), F32).at[:, LR_LANE:LR_LANE + G_RANK, :].set(w_lr2)
    row = lambda x: x.reshape(DEPTH, 1, x.shape[-1])
    w_in_t = jnp.swapaxes(w_in, 1, 2)
    return dict(
        norm_mix_g=row(norm_mix_g), w_big=_repack_w_in(w_in_t), w_small=_narrow_w_in(w_in_t),
        gate_bias=row(gate_bias), w_lr2=w_lr2_pad.astype(BF16), b_lr2=row(b_lr2),
        g_mnorm=row(g_mnorm), g_gnorm=row(g_gnorm),
        w_pa=w_pa.astype(BF16), w_pb=w_pb.astype(BF16), w_o=w_o.astype(BF16),
        norm_ffn_g=row(norm_ffn_g), w_up=w_up.astype(BF16), conv_w=conv_w, conv_b=row(conv_b),
        w_down=w_down.astype(BF16), norm_ple_g=row(norm_ple_g),
        w_ple_gate=w_ple_gate.astype(BF16), w_ple_proj=w_ple_proj.astype(BF16),
        final_norm_g=final_norm_g.reshape(1, D_MODEL),
    )


def _trunk(x, p, states, stacked, cv0, W, *, seq_rows):
    T = x.shape[0]
    B = T // seq_rows
    if seq_rows >= 2 * PROMPT_CHUNK:
        L, seg = PROMPT_CHUNK, PROMPT_CHUNK
    else:
        L, seg = SAMPLE_GROUP * seq_rows, seq_rows
    h = x
    new_states = None
    cvs = []
    for layer in range(DEPTH):
        hm, hg, zg, new_states = _front(h, states, layer if stacked else 0, new_states, layer, W,
                                        L=L, seg=seg, seq_rows=seq_rows)
        if cv0 is None:
            pst = None
        else:
            pst = jnp.pad(cv0[layer], ((0, 0), (0, seq_rows - (CONV_W - 1)), (0, 0))).reshape(T, D_FF)
        h, cv = _channel(h, hm, hg, zg, p, pst, layer, W, seq_rows=seq_rows, final=layer == DEPTH - 1)
        if cv0 is None:
            cv = cv.reshape(B, -1, SUBLANE, D_FF)[:, -1, SUBLANE - (CONV_W - 1):, :]
        else:
            cv = cv.reshape(B, seq_rows, D_FF)[:, seq_rows - (CONV_W - 1):, :]
        cvs.append(cv)
    c, n, m, s = new_states
    return h, c, n, m.reshape(DEPTH, B, HEADS), s, jnp.stack(cvs)


def kernel(x_prompt, x_sample, state_mlstm_C, state_mlstm_n, state_mlstm_m, state_gla_S, state_ffn_conv, p_prompt, p_sample, norm_mix_g, w_in, b_mi, b_mf, g_mnorm, w_lr2, b_lr2, g_gnorm, w_pa, w_pb, w_o, norm_ffn_g, w_up, conv_w, conv_b, w_down, norm_ple_g, w_ple_gate, w_ple_proj, final_norm_g):
    W = _prep_weights(norm_mix_g, w_in, b_mi, b_mf, g_mnorm, w_lr2, b_lr2, g_gnorm, w_pa, w_pb, w_o,
                      norm_ffn_g, w_up, conv_w, conv_b, w_down, norm_ple_g, w_ple_gate, w_ple_proj,
                      final_norm_g)
    bp, sp, _ = x_prompt.shape
    bs, ss, _ = x_sample.shape
    zero_states = (jnp.zeros((1, bp, HEADS, DK, DV), F32), jnp.zeros((1, bp, HEADS, DK), F32),
                   jnp.zeros((1, bp, 1, HEADS), F32), jnp.zeros((1, bp, HEADS, DK, DV), F32))
    yp, c_p, n_p, m_p, s_p, cv_p = _trunk(
        x_prompt.reshape(bp * sp, D_MODEL), p_prompt.reshape(DEPTH, bp * sp, PLE_DIM),
        zero_states, False, None, W, seq_rows=sp)
    sample_states = (state_mlstm_C, state_mlstm_n, state_mlstm_m.reshape(DEPTH, bs, 1, HEADS), state_gla_S)
    ys, c_s, n_s, m_s, s_s, cv_s = _trunk(
        x_sample.reshape(bs * ss, D_MODEL), p_sample.reshape(DEPTH, bs * ss, PLE_DIM),
        sample_states, True, state_ffn_conv, W, seq_rows=ss)
    return (yp.reshape(bp, sp, D_MODEL), ys.reshape(bs, ss, D_MODEL),
            c_p, n_p, m_p, s_p, cv_p, c_s, n_s, m_s, s_s, cv_s)
```

```python
import functools

import jax
import jax.numpy as jnp
from jax import lax
from jax.experimental import pallas as pl
from jax.experimental.pallas import tpu as pltpu

F32 = jnp.float32
BF16 = jnp.bfloat16
HIGHEST = lax.Precision.HIGHEST

D_MODEL = 1024
DEPTH = 4
HEADS = 4
DK = 128
DV = 256
G_RANK = 16
G_TAU = 16.0
D_FF = 2688
CONV_W = 3
PLE_DIM = 256
EPS = 1e-6
SCALE_K = DK ** -0.5

MQ, MK, MV, MO = 0, 512, 1024, 2048
GQ, GK, GV, GG = 3072, 3584, 4096, 5120
GA, GB = 6144, 7168
Z_BIG = 8192
Z_MIX = 6144
Z_GATES = Z_BIG - Z_MIX
Z_SMALL = 128
LR_LANE = 8
SRC_MI, SRC_GLR = 2048, 5128
SHIFT_AFTER_MI = 2 * HEADS
SHIFT_AFTER_GLR = 2 * HEADS + G_RANK

LANE = 128
SUBLANE = 8
VMEM_LIMIT_BYTES = 60 * 1024 * 1024

PROJ_ROWS = 512
PROJ_COLS = 256
CHAN_ROWS = 256
CHAN_SPLIT = 2
UP_CHUNK = 1792
FF_CHUNK = 896
REPACK_COLS = 1024
PROMPT_CHUNK = 256
SAMPLE_GROUP = 8
GLA_SAFE_LOG_DECAY = -60.0


def _dot(a, b):
    return jnp.dot(a, b, preferred_element_type=F32)


def _dot_nt(a, b):
    return lax.dot_general(a, b, (((1,), (1,)), ((), ())), preferred_element_type=F32)


def _dot_tn(a, b):
    return lax.dot_general(a, b, (((0,), (0,)), ((), ())), preferred_element_type=F32)


def _log_sigmoid(x):
    return jnp.minimum(x, 0.0) - jnp.log(1.0 + jnp.exp(-jnp.abs(x)))


def _masked_cumsum(mask_bf16, x, pieces):
    acc = None
    rest = x
    for n in range(pieces):
        piece = rest.astype(BF16)
        part = _dot(mask_bf16, piece)
        acc = part if acc is None else acc + part
        if n + 1 < pieces:
            rest = rest - piece.astype(F32)
    return acc


def _sigmoid(x):
    return 0.5 * jnp.tanh(0.5 * x) + 0.5


def _rmsnorm(x, g):
    return x * lax.rsqrt(jnp.mean(x * x, axis=-1, keepdims=True) + EPS) * g


def _resident(block_shape, index_map):
    return pl.BlockSpec(block_shape, index_map, pipeline_mode=pl.Buffered(1))


REPACK_TAIL = 32


def _repack_kernel(a_ref, b_ref, o_ref):
    t = pl.program_id(1)
    first_after_mi = MO // REPACK_COLS
    first_after_glr = GG // REPACK_COLS

    def emit(shift):
        if shift == 0:
            y = a_ref[...]
        else:
            x = jnp.concatenate([a_ref[...], b_ref[...]], axis=0)
            y = x[shift:shift + REPACK_COLS, :]
        o_ref[...] = y.T.astype(BF16)

    @pl.when(t < first_after_mi)
    def _():
        emit(0)

    @pl.when((t >= first_after_mi) & (t < first_after_glr))
    def _():
        emit(SHIFT_AFTER_MI)

    @pl.when(t >= first_after_glr)
    def _():
        emit(SHIFT_AFTER_GLR)


def _repack_w_in(w_in_t):
    tails_per_tile = REPACK_COLS // REPACK_TAIL
    return pl.pallas_call(
        _repack_kernel,
        grid=(DEPTH, Z_BIG // REPACK_COLS),
        in_specs=[
            pl.BlockSpec((None, REPACK_COLS, D_MODEL), lambda l, t: (l, t, 0)),
            pl.BlockSpec((None, REPACK_TAIL, D_MODEL), lambda l, t: (l, tails_per_tile * (t + 1), 0)),
        ],
        out_specs=pl.BlockSpec((None, D_MODEL, REPACK_COLS), lambda l, t: (l, 0, t)),
        out_shape=jax.ShapeDtypeStruct((DEPTH, D_MODEL, Z_BIG), BF16),
        compiler_params=pltpu.CompilerParams(
            dimension_semantics=("arbitrary", "arbitrary"),
            vmem_limit_bytes=VMEM_LIMIT_BYTES,
        ),
        name="repack_w_in",
    )(w_in_t, w_in_t)


def _narrow_kernel(mi_ref, glr0_ref, glr1_ref, o_ref):
    pad = jnp.zeros((Z_SMALL - SHIFT_AFTER_GLR, D_MODEL), F32)
    x = jnp.concatenate([mi_ref[...], glr0_ref[...], glr1_ref[...], pad], axis=0)
    o_ref[...] = x.T.astype(BF16)


def _narrow_w_in(w_in_t):
    assert SRC_MI % SUBLANE == 0 and SRC_GLR % SUBLANE == 0 and G_RANK == 2 * SUBLANE
    rows8 = lambda r: pl.BlockSpec((None, SUBLANE, D_MODEL), lambda l: (l, r // SUBLANE, 0))
    return pl.pallas_call(
        _narrow_kernel,
        grid=(DEPTH,),
        in_specs=[rows8(SRC_MI), rows8(SRC_GLR), rows8(SRC_GLR + SUBLANE)],
        out_specs=pl.BlockSpec((None, D_MODEL, Z_SMALL), lambda l: (l, 0, 0)),
        out_shape=jax.ShapeDtypeStruct((DEPTH, D_MODEL, Z_SMALL), BF16),
        compiler_params=pltpu.CompilerParams(dimension_semantics=("arbitrary",)),
        name="narrow_w_in",
    )(w_in_t, w_in_t, w_in_t)


def _in_proj_kernel(h_ref, g_ref, wbig_ref, wsm_ref, zbig_ref, zsm_ref):
    a = _rmsnorm(h_ref[...], g_ref[...]).astype(BF16)
    zsm_ref[...] = _dot(a, wsm_ref[...])
    for n in range(Z_BIG // PROJ_COLS):
        cols = slice(n * PROJ_COLS, (n + 1) * PROJ_COLS)
        zbig_ref[:, cols] = _dot(a, wbig_ref[:, cols])


def _in_proj(h, layer, W):
    T = h.shape[0]
    par = lambda i: (layer, 0, 0)
    return pl.pallas_call(
        _in_proj_kernel,
        grid=(T // PROJ_ROWS,),
        in_specs=[
            pl.BlockSpec((PROJ_ROWS, D_MODEL), lambda i: (i, 0)),
            _resident((None, 1, D_MODEL), par),
            _resident((None, D_MODEL, Z_BIG), par),
            _resident((None, D_MODEL, Z_SMALL), par),
        ],
        out_specs=[
            pl.BlockSpec((PROJ_ROWS, Z_BIG), lambda i: (i, 0)),
            pl.BlockSpec((PROJ_ROWS, Z_SMALL), lambda i: (i, 0)),
        ],
        out_shape=[
            jax.ShapeDtypeStruct((T, Z_BIG), F32),
            jax.ShapeDtypeStruct((T, Z_SMALL), F32),
        ],
        compiler_params=pltpu.CompilerParams(
            dimension_semantics=("arbitrary",),
            vmem_limit_bytes=VMEM_LIMIT_BYTES,
        ),
        name="in_proj",
    )(h, W["norm_mix_g"], W["w_big"], W["w_small"])


MIX_SLOTS = 1 + 8 * HEADS


def _project_stages(h_ref, r0, rows, g_ref, wbig_ref, wsm_ref, z_ref, zs_ref):
    held = {}

    def head():
        a = _rmsnorm(h_ref[r0:r0 + rows, :], g_ref[...]).astype(BF16)
        held["a"] = a
        zs_ref[...] = _dot(a, wsm_ref[...])

    def body(n):
        cols = slice(n * PROJ_COLS, (n + 1) * PROJ_COLS)
        z_ref[:, cols] = _dot(held["a"], wbig_ref[:, cols])

    return [head] + [functools.partial(body, n) for n in range(Z_BIG // PROJ_COLS)]


def _anchor(x, half):
    rows, width = x.shape
    if half >= SUBLANE:
        parts = []
        for blk in range(rows // (2 * half)):
            r = blk * 2 * half + half - 1
            parts.append(jnp.broadcast_to(x[r:r + 1, :], (2 * half, width)))
        return parts[0] if len(parts) == 1 else jnp.concatenate(parts, axis=0)
    x3 = x.reshape(rows // SUBLANE, SUBLANE, width)
    sub = lax.broadcasted_iota(jnp.int32, x3.shape, 1)

    def bc(i):
        return jnp.broadcast_to(x3[:, i:i + 1, :], x3.shape)

    if half == 4:
        out = bc(3)
    elif half == 2:
        out = jnp.where(sub < 4, bc(1), bc(5))
    else:
        out = jnp.where(sub < 2, bc(0), jnp.where(sub < 4, bc(2), jnp.where(sub < 6, bc(4), bc(6))))
    return out.reshape(rows, width)


def _causal_mask(L, seg):
    row = lax.broadcasted_iota(jnp.int32, (L, L), 0)
    col = lax.broadcasted_iota(jnp.int32, (L, L), 1)
    return (col <= row) & ((row ^ col) < seg)


def _mix_chunk(zb_ref, zs_ref, r0, state_in, params, outs, *, L, seg, make_fillers=None):
    w2_ref, blr_ref = params[1:3]
    tril_b = _causal_mask(L, seg).astype(F32).astype(BF16)
    la = _log_sigmoid(_dot(zs_ref[...].astype(BF16), w2_ref[...]) + blr_ref[...]) * (1.0 / G_TAU)
    bc = _masked_cumsum(tril_b, la, 2)

    def run(single_anchor):
        fillers = make_fillers() if make_fillers else ()
        _mix_rest(zb_ref, zs_ref, bc, r0, state_in, params, outs, L=L, seg=seg,
                  single_anchor=single_anchor, fillers=fillers)

    lax.cond(jnp.min(bc) >= GLA_SAFE_LOG_DECAY, lambda: run(True), lambda: run(False))


def _mix_rest(zb_ref, zs_ref, bc, r0, state_in, params, outs, *, L, seg, single_anchor, fillers):
    cin, nin, min_, sin = state_in
    bias_ref, w2_ref, blr_ref, gm_ref, gg_ref = params
    hm_ref, hg_ref, zg_ref, c_ref, n_ref, m_ref, s_ref = outs
    orow = slice(r0, r0 + L)
    nseg = L // seg
    seg_shift = seg.bit_length() - 1

    row = lax.broadcasted_iota(jnp.int32, (L, L), 0)
    col = lax.broadcasted_iota(jnp.int32, (L, L), 1)
    rxc = row ^ col
    causal = (col <= row) & (rxc < seg)
    tril_b = causal.astype(F32).astype(BF16)
    lane = lax.broadcasted_iota(jnp.int32, (L, LANE), 1)
    rowl = lax.broadcasted_iota(jnp.int32, (L, LANE), 0)
    eye_k = (lax.broadcasted_iota(jnp.int32, (DK, DK), 0)
             == lax.broadcasted_iota(jnp.int32, (DK, DK), 1))
    lane4 = lax.broadcasted_iota(jnp.int32, (1, HEADS), 1)

    def seg_last(x):
        parts = [jnp.broadcast_to(x[(g + 1) * seg - 1:(g + 1) * seg, :], (seg, x.shape[1]))
                 for g in range(nseg)]
        return parts[0] if nseg == 1 else jnp.concatenate(parts, axis=0)

    def seg_rows(vals):
        parts = [jnp.broadcast_to(v, (seg, v.shape[1])) for v in vals]
        return parts[0] if nseg == 1 else jnp.concatenate(parts, axis=0)

    def seg_mask(x, g):
        return x if nseg == 1 else jnp.where((rowl >> seg_shift) == g, x, 0.0)

    def per_seg_dot(x_f32, state_ref, h):
        parts = [_dot(x_f32[g * seg:(g + 1) * seg, :].astype(BF16), state_ref[g, h].astype(BF16))
                 for g in range(nseg)]
        return parts[0] if nseg == 1 else jnp.concatenate(parts, axis=0)

    fillers = list(fillers)

    per_slot = -(-len(fillers) // MIX_SLOTS)

    def next_filler():
        for _ in range(min(per_slot, len(fillers))):
            fillers.pop(0)()

    next_filler()
    zg_ref[orow, :] = zb_ref[:, Z_MIX:].astype(BF16)

    zs = zs_ref[...]
    gates = zs + bias_ref[...]
    lf = _log_sigmoid(gates)
    cum = _masked_cumsum(tril_b, lf, 3)
    cols = jnp.where(lane < HEADS, gates, cum)
    sel = (lax.broadcasted_iota(jnp.int32, (SUBLANE, LANE), 0)
           == lax.broadcasted_iota(jnp.int32, (SUBLANE, LANE), 1)).astype(F32)
    rows_ = lax.dot_general(sel, cols, (((1,), (1,)), ((), ())),
                            precision=HIGHEST, preferred_element_type=F32)

    def lhs_with_state(a_intra, x_inter):
        return jnp.concatenate([a_intra.astype(BF16), x_inter.astype(BF16)], axis=1)

    def rhs_with_state(v, state_ref, h):
        return jnp.concatenate([v, state_ref[0, h].astype(BF16)], axis=0)

    m_old = [min_[g] for g in range(nseg)]
    m_new = [jnp.zeros((1, HEADS), F32) for _ in range(nseg)]

    heads = range(HEADS)

    qf = [zb_ref[:, MQ + DK * h:MQ + DK * (h + 1)] for h in heads]
    kf = [zb_ref[:, MK + DK * h:MK + DK * (h + 1)] for h in heads]
    vm = [zb_ref[:, MV + DV * h:MV + DV * (h + 1)].astype(BF16) for h in heads]
    li_c = [cols[:, h:h + 1] for h in heads]
    b_c = [cols[:, HEADS + h:HEADS + h + 1] for h in heads]
    mt, w_inter, amat_m = [], [], []
    for h in heads:
        li_r = rows_[h:h + 1, :]
        b_r = rows_[HEADS + h:HEADS + h + 1, :]
        m_col = seg_rows([m_old[g][:, h:h + 1] for g in range(nseg)])
        dmat = jnp.where(causal, b_c[h] + (li_r - b_r), -jnp.inf)
        inter = b_c[h] + m_col
        mt.append(jnp.maximum(inter, jnp.max(dmat, axis=1, keepdims=True)))
        w_inter.append(jnp.exp(inter - mt[h]))
        scores = _dot_nt(qf[h].astype(BF16), kf[h].astype(BF16))
        amat_m.append(jnp.exp(dmat - mt[h]) * (scores * SCALE_K))
        next_filler()

    qs = [zb_ref[:, GQ + DK * h:GQ + DK * (h + 1)] for h in heads]
    ks = [zb_ref[:, GK + DK * h:GK + DK * (h + 1)] for h in heads]
    vg = [zb_ref[:, GV + DV * h:GV + DV * (h + 1)].astype(BF16) for h in heads]
    b_h = [bc[:, DK * h:DK * (h + 1)] for h in heads]
    qe = [qs[h] * jnp.exp(b_h[h]) for h in heads]

    def intra_single_anchor():
        out = []
        for h in heads:
            ke = ks[h] * jnp.exp(-b_h[h])
            out.append(jnp.where(causal, _dot_nt(qe[h].astype(BF16), ke.astype(BF16)), 0.0))
            next_filler()
        return out

    def intra_by_levels():
        out = [jnp.where(row == col, jnp.sum(qs[h] * ks[h], axis=1, keepdims=True), 0.0)
               for h in heads]
        half = 1
        while half < seg:
            wgt = jnp.exp(-jnp.abs(bc - _anchor(bc, half)))
            pair = (col < row) & (rxc >= half) & (rxc < 2 * half)
            for h in heads:
                w_h = wgt[:, DK * h:DK * (h + 1)]
                lvl = _dot_nt((qs[h] * w_h).astype(BF16), (ks[h] * w_h).astype(BF16))
                out[h] = jnp.where(pair, lvl, out[h])
            half *= 2
        for h in heads:
            next_filler()
        return out

    amat_g = intra_single_anchor() if single_anchor else intra_by_levels()

    hh = []
    for h in heads:
        n_rows = seg_rows([nin[g, h:h + 1, :] for g in range(nseg)])
        qn = jnp.sum(qf[h] * n_rows, axis=1, keepdims=True)
        if nseg == 1:
            num = _dot(lhs_with_state(amat_m[h], w_inter[h] * qf[h]), rhs_with_state(vm[h], cin, h))
        else:
            num = _dot(amat_m[h].astype(BF16), vm[h]) + w_inter[h] * per_seg_dot(qf[h], cin, h)
        den = jnp.sum(amat_m[h], axis=1, keepdims=True) + w_inter[h] * qn
        den = jnp.maximum(jnp.abs(den), jnp.exp(-mt[h]))
        hh.append(num * (1.0 / den))
        next_filler()

    og_ = []
    for h in heads:
        if nseg == 1:
            o = _dot(lhs_with_state(amat_g[h] * SCALE_K, qe[h] * SCALE_K), rhs_with_state(vg[h], sin, h))
        else:
            o = _dot((amat_g[h] * SCALE_K).astype(BF16), vg[h]) + SCALE_K * per_seg_dot(qe[h], sin, h)
        og_.append(o)
        next_filler()

    for h in heads:
        mu = jnp.mean(hh[h], axis=-1, keepdims=True)
        hc = hh[h] - mu
        hn = hc * lax.rsqrt(jnp.mean(hc * hc, axis=-1, keepdims=True) + EPS)
        og = zb_ref[:, MO + DV * h:MO + DV * (h + 1)]
        hm_ref[orow, DV * h:DV * (h + 1)] = (hn * gm_ref[:, DV * h:DV * (h + 1)] * _sigmoid(og)).astype(BF16)
        next_filler()

    for h in heads:
        o = og_[h]
        on = o * lax.rsqrt(jnp.mean(o * o, axis=-1, keepdims=True) + EPS)
        gg = zb_ref[:, GG + DV * h:GG + DV * (h + 1)]
        hg_ref[orow, DV * h:DV * (h + 1)] = (on * gg_ref[:, DV * h:DV * (h + 1)] * (gg * _sigmoid(gg))).astype(BF16)
        next_filler()

    for h in heads:
        ws = jnp.exp(seg_last(b_c[h]) - b_c[h] + li_c[h] - seg_last(mt[h])) * SCALE_K
        kw = kf[h] * ws
        for g in range(nseg):
            last = (g + 1) * seg - 1
            m_last = mt[h][last:last + 1, :]
            cw = jnp.exp(b_c[h][last:last + 1, :] + m_old[g][:, h:h + 1] - m_last)
            c_new = cw * cin[g, h] + _dot_tn(seg_mask(kw, g).astype(BF16), vm[h])
            n_new = cw * nin[g, h:h + 1, :] + jnp.sum(kw[g * seg:(g + 1) * seg, :], axis=0, keepdims=True)
            c_ref[g, h] = c_new
            n_ref[g, h:h + 1, :] = n_new
            m_new[g] = jnp.where(lane4 == h, m_last, m_new[g])
        next_filler()
    for g in range(nseg):
        m_ref[g] = m_new[g]

    bl_rows = seg_last(bc)
    for h in heads:
        kt = ks[h] * jnp.exp(bl_rows[:, DK * h:DK * (h + 1)] - b_h[h])
        for g in range(nseg):
            last = (g + 1) * seg - 1
            dec_row = jnp.exp(b_h[h][last:last + 1, :])
            dec_col = jnp.sum(jnp.where(eye_k, jnp.broadcast_to(dec_row, (DK, DK)), 0.0),
                              axis=1, keepdims=True)
            s_ref[g, h] = dec_col * sin[g, h] + _dot_tn(seg_mask(kt, g).astype(BF16), vg[h])
        next_filler()
    assert not fillers


def _front_kernel(*refs, L, seg, pipelined, aliased):
    lead = 5 if pipelined else 2
    pos = lead
    c0_ref, n0_ref, m0_ref, s0_ref = refs[pos:pos + 4]
    pos += 4
    params = refs[pos:pos + 5]
    pos += 5 + (4 if aliased else 0)
    outs = refs[pos:pos + 7]
    pos += 7
    scratch = refs[pos:]
    c_ref, n_ref, m_ref, s_ref = outs[3:]

    if not pipelined:
        z_ref, zs_ref = refs[:lead]
        _mix_chunk(z_ref, zs_ref, 0, (c0_ref, n0_ref, m0_ref, s0_ref), params, outs, L=L, seg=seg)
        return

    h_cur, h_nxt, g_ref, wbig_ref, wsm_ref = refs[:lead]
    stages = functools.partial(_project_stages, g_ref=g_ref, wbig_ref=wbig_ref, wsm_ref=wsm_ref)
    z_a, zs_a, z_b, zs_b = scratch

    @pl.when(pl.program_id(1) == 0)
    def _():
        c_ref[...] = c0_ref[...]
        n_ref[...] = n0_ref[...]
        m_ref[...] = m0_ref[...]
        s_ref[...] = s0_ref[...]

    @pl.when((pl.program_id(0) == 0) & (pl.program_id(1) == 0))
    def _():
        for stage in stages(h_cur, 0, L, z_ref=z_a, zs_ref=zs_a):
            stage()

    carried = (c_ref, n_ref, m_ref, s_ref)
    _mix_chunk(z_a, zs_a, 0, carried, params, outs, L=L, seg=seg,
               make_fillers=lambda: stages(h_cur, L, L, z_ref=z_b, zs_ref=zs_b))
    _mix_chunk(z_b, zs_b, L, carried, params, outs, L=L, seg=seg,
               make_fillers=lambda: stages(h_nxt, 0, L, z_ref=z_a, zs_ref=zs_a))


def _front(h, states_in, in_layer, states_out, layer, W, *, L, seg, seq_rows):
    T = h.shape[0]
    nseg = L // seg
    B = states_in[0].shape[1]
    pipelined = seq_rows > L
    rows = 2 * L if pipelined else L
    steps = seq_rows // rows if pipelined else 1
    grid = (B // nseg, steps)
    row_map = lambda b, k: (b * steps + k, 0)
    last_step = (B // nseg) * steps - 1
    nxt_map = lambda b, k: (2 * jnp.minimum(b * steps + k + 1, last_step), 0)
    par = lambda b, k: (layer, 0, 0)

    def state_specs(at):
        return [
            pl.BlockSpec((None, nseg, HEADS, DK, DV), lambda b, k: (at, b, 0, 0, 0)),
            pl.BlockSpec((None, nseg, HEADS, DK), lambda b, k: (at, b, 0, 0)),
            pl.BlockSpec((None, nseg, 1, HEADS), lambda b, k: (at, b, 0, 0)),
            pl.BlockSpec((None, nseg, HEADS, DK, DV), lambda b, k: (at, b, 0, 0, 0)),
        ]

    if pipelined:
        in_specs = [
            pl.BlockSpec((rows, D_MODEL), row_map),
            pl.BlockSpec((L, D_MODEL), nxt_map),
            _resident((None, 1, D_MODEL), par),
            _resident((None, D_MODEL, Z_BIG), par),
            _resident((None, D_MODEL, Z_SMALL), par),
        ]
        args = [h, h, W["norm_mix_g"], W["w_big"], W["w_small"]]
    else:
        in_specs = [pl.BlockSpec((rows, Z_BIG), row_map), pl.BlockSpec((rows, Z_SMALL), row_map)]
        args = list(_in_proj(h, layer, W))
    in_specs += [
        *state_specs(in_layer),
        _resident((None, 1, Z_SMALL), par),
        _resident((None, Z_SMALL, HEADS * DK), par),
        _resident((None, 1, HEADS * DK), par),
        _resident((None, 1, HEADS * DV), par),
        _resident((None, 1, HEADS * DV), par),
    ]
    args += [*states_in, W["gate_bias"], W["w_lr2"], W["b_lr2"], W["g_mnorm"], W["g_gnorm"]]
    aliases = {}
    if states_out is not None:
        for n, s in enumerate(states_out):
            aliases[len(args)] = 3 + n
            in_specs.append(pl.BlockSpec(memory_space=pl.ANY))
            args.append(s)
    out_shape = [
        jax.ShapeDtypeStruct((T, HEADS * DV), BF16),
        jax.ShapeDtypeStruct((T, HEADS * DV), BF16),
        jax.ShapeDtypeStruct((T, Z_GATES), BF16),
        jax.ShapeDtypeStruct((DEPTH, B, HEADS, DK, DV), F32),
        jax.ShapeDtypeStruct((DEPTH, B, HEADS, DK), F32),
        jax.ShapeDtypeStruct((DEPTH, B, 1, HEADS), F32),
        jax.ShapeDtypeStruct((DEPTH, B, HEADS, DK, DV), F32),
    ]
    z_scratch = [pltpu.VMEM((L, Z_BIG), F32), pltpu.VMEM((L, Z_SMALL), F32)]
    outs = pl.pallas_call(
        functools.partial(_front_kernel, L=L, seg=seg, pipelined=pipelined, aliased=states_out is not None),
        grid=grid,
        in_specs=in_specs,
        out_specs=[
            pl.BlockSpec((rows, HEADS * DV), row_map),
            pl.BlockSpec((rows, HEADS * DV), row_map),
            pl.BlockSpec((rows, Z_GATES), row_map),
            *state_specs(layer),
        ],
        out_shape=out_shape,
        input_output_aliases=aliases,
        scratch_shapes=z_scratch * 2 if pipelined else [],
        compiler_params=pltpu.CompilerParams(
            dimension_semantics=("arbitrary", "arbitrary"),
            vmem_limit_bytes=VMEM_LIMIT_BYTES,
        ),
        name="front",
    )(*args)
    return outs[0], outs[1], outs[2], tuple(outs[3:])


def _channel_kernel(*refs, seq_tiles, has_state, final):
    (h_ref, hm_ref, hg_ref, ga_ref, gb_ref, wpa_ref, wpb_ref, wo_ref, gffn_ref,
     wup_ref, cw_ref, cb_ref, wd_ref, gple_ref, wpg_ref, wpp_ref, p_ref) = refs[:17]
    pos = 17
    pst_ref = gfin_ref = carry_scr = None
    if has_state:
        pst_ref = refs[pos]
        pos += 1
    if final:
        gfin_ref = refs[pos]
        pos += 1
    out_ref, conv_ref = refs[pos:pos + 2]
    if not has_state:
        carry_scr = refs[pos + 2]

    i = pl.program_id(0)
    tm = h_ref.shape[0]
    hr = tm // CHAN_SPLIT
    groups = [slice(r * hr, (r + 1) * hr) for r in range(CHAN_SPLIT)]

    pa = [_dot(hm_ref[rs, :], wpa_ref[...]) for rs in groups]
    pb = [_dot(hg_ref[rs, :], wpb_ref[...]) for rs in groups]
    h1, c = [], []
    for r, rs in enumerate(groups):
        merged = (_sigmoid(ga_ref[rs, :].astype(F32)) * pa[r]
                  + _sigmoid(gb_ref[rs, :].astype(F32)) * pb[r])
        h1.append(h_ref[rs, :] + _dot(merged.astype(BF16), wo_ref[...]))
        c.append(_rmsnorm(h1[r], gffn_ref[...]).astype(BF16))

    up = [[None] * CHAN_SPLIT for _ in range(2 * D_FF // UP_CHUNK)]
    for n in range(2 * D_FF // UP_CHUNK):
        for r in range(CHAN_SPLIT):
            up[n][r] = _dot(c[r], wup_ref[:, n * UP_CHUNK:(n + 1) * UP_CHUNK])

    def up_cols(start, r):
        n, off = divmod(start, UP_CHUNK)
        assert off + FF_CHUNK <= UP_CHUNK
        return up[n][r][:, off:off + FF_CHUNK]

    rowi = lax.broadcasted_iota(jnp.int32, (hr, FF_CHUNK), 0)
    acc = [None] * CHAN_SPLIT
    for n in range(D_FF // FF_CHUNK):
        ff = slice(n * FF_CHUNK, (n + 1) * FF_CHUNK)
        if not has_state:
            prev = jnp.where(i % seq_tiles == 0, 0.0, carry_scr[:, ff])
        for r, rs in enumerate(groups):
            ug = up_cols(n * FF_CHUNK, r)
            uv = up_cols(D_FF + n * FF_CHUNK, r)
            r1 = pltpu.roll(ug, 1, 0)
            r2 = pltpu.roll(ug, 2, 0)
            if has_state:
                pst = pst_ref[rs, ff]
                t = rowi & (SUBLANE - 1)
                u1 = jnp.where(t == 0, pltpu.roll(pst, hr - 1, 0), r1)
                u2 = jnp.where(t < 2, pst, r2)
                conv_ref[rs, ff] = ug
            else:
                p6 = jnp.broadcast_to(prev[6:7, :], ug.shape)
                p7 = jnp.broadcast_to(prev[7:8, :], ug.shape)
                u1 = jnp.where(rowi == 0, p7, r1)
                u2 = jnp.where(rowi == 0, p6, jnp.where(rowi == 1, p7, r2))
                prev = ug[hr - SUBLANE:, :]
            conv = cb_ref[:, ff] + cw_ref[0:1, ff] * u2 + cw_ref[1:2, ff] * u1 + cw_ref[2:3, ff] * ug
            act = conv * _sigmoid(conv) * uv
            part = _dot(act.astype(BF16), wd_ref[ff, :])
            acc[r] = part if acc[r] is None else acc[r] + part
        if not has_state:
            carry_scr[:, ff] = prev
            conv_ref[:, ff] = prev

    for r, rs in enumerate(groups):
        h2 = h1[r] + acc[r]
        e = _rmsnorm(h2, gple_ref[...]).astype(BF16)
        h3 = h2 + _sigmoid(_dot(e, wpg_ref[...])) * _dot(p_ref[rs, :].astype(BF16), wpp_ref[...])
        if final:
            h3 = _rmsnorm(h3, gfin_ref[...])
        out_ref[rs, :] = h3


def _channel(h, hm, hg, zg, p, pst, layer, W, *, seq_rows, final):
    T = h.shape[0]
    tm = CHAN_ROWS
    has_state = pst is not None
    seq_tiles = max(seq_rows // tm, 1)
    rows = lambda i: (i, 0)
    par = lambda i: (layer, 0, 0)
    in_specs = [
        pl.BlockSpec((tm, D_MODEL), rows),
        pl.BlockSpec((tm, D_MODEL), rows),
        pl.BlockSpec((tm, D_MODEL), rows),
        pl.BlockSpec((tm, D_MODEL), lambda i: (i, (GA - Z_MIX) // D_MODEL)),
        pl.BlockSpec((tm, D_MODEL), lambda i: (i, (GB - Z_MIX) // D_MODEL)),
        _resident((None, D_MODEL, D_MODEL), par),
        _resident((None, D_MODEL, D_MODEL), par),
        _resident((None, D_MODEL, D_MODEL), par),
        _resident((None, 1, D_MODEL), par),
        _resident((None, D_MODEL, 2 * D_FF), par),
        _resident((None, CONV_W, D_FF), par),
        _resident((None, 1, D_FF), par),
        _resident((None, D_FF, D_MODEL), par),
        _resident((None, 1, D_MODEL), par),
        _resident((None, D_MODEL, D_MODEL), par),
        _resident((None, PLE_DIM, D_MODEL), par),
        pl.BlockSpec((None, tm, PLE_DIM), lambda i: (layer, i, 0)),
    ]
    args = [h, hm, hg, zg, zg, W["w_pa"], W["w_pb"], W["w_o"], W["norm_ffn_g"],
            W["w_up"], W["conv_w"], W["conv_b"], W["w_down"], W["norm_ple_g"],
            W["w_ple_gate"], W["w_ple_proj"], p]
    scratch = []
    if has_state:
        in_specs.append(pl.BlockSpec((tm, D_FF), rows))
        args.append(pst)
        conv_spec = pl.BlockSpec((tm, D_FF), rows)
        conv_shape = jax.ShapeDtypeStruct((T, D_FF), F32)
    else:
        conv_spec = pl.BlockSpec((None, SUBLANE, D_FF), lambda i: (i, 0, 0))
        conv_shape = jax.ShapeDtypeStruct((T // tm, SUBLANE, D_FF), F32)
        scratch.append(pltpu.VMEM((SUBLANE, D_FF), F32))
    if final:
        in_specs.append(_resident((1, D_MODEL), lambda i: (0, 0)))
        args.append(W["final_norm_g"])
    return pl.pallas_call(
        functools.partial(_channel_kernel, seq_tiles=seq_tiles, has_state=has_state, final=final),
        grid=(T // tm,),
        in_specs=in_specs,
        out_specs=[pl.BlockSpec((tm, D_MODEL), rows), conv_spec],
        out_shape=[jax.ShapeDtypeStruct((T, D_MODEL), F32), conv_shape],
        scratch_shapes=scratch,
        compiler_params=pltpu.CompilerParams(
            dimension_semantics=("arbitrary",),
            vmem_limit_bytes=VMEM_LIMIT_BYTES,
        ),
        name="channel",
    )(*args)


def _prep_weights(norm_mix_g, w_in, b_mi, b_mf, g_mnorm, w_lr2, b_lr2, g_gnorm, w_pa, w_pb, w_o,
                  norm_ffn_g, w_up, conv_w, conv_b, w_down, norm_ple_g, w_ple_gate, w_ple_proj,
                  final_norm_g):
    gate_bias = jnp.concatenate([b_mi, b_mf, jnp.zeros((DEPTH, Z_SMALL - 2 * HEADS), F32)], axis=-1)
    w_lr2_pad = jnp.zeros((DEPTH, Z_SMALL, HEADS * DK), F32).at[:, LR_LANE:LR_LANE + G_RANK, :].set(w_lr2)
    row = lambda x: x.reshape(DEPTH, 1, x.shape[-1])
    w_in_t = jnp.swapaxes(w_in, 1, 2)
    return dict(
        norm_mix_g=row(norm_mix_g), w_big=_repack_w_in(w_in_t), w_small=_narrow_w_in(w_in_t),
        gate_bias=row(gate_bias), w_lr2=w_lr2_pad.astype(BF16), b_lr2=row(b_lr2),
        g_mnorm=row(g_mnorm), g_gnorm=row(g_gnorm),
        w_pa=w_pa.astype(BF16), w_pb=w_pb.astype(BF16), w_o=w_o.astype(BF16),
        norm_ffn_g=row(norm_ffn_g), w_up=w_up.astype(BF16), conv_w=conv_w, conv_b=row(conv_b),
        w_down=w_down.astype(BF16), norm_ple_g=row(norm_ple_g),
        w_ple_gate=w_ple_gate.astype(BF16), w_ple_proj=w_ple_proj.astype(BF16),
        final_norm_g=final_norm_g.reshape(1, D_MODEL),
    )


def _trunk(x, p, states, stacked, cv0, W, *, seq_rows):
    T = x.shape[0]
    B = T // seq_rows
    if seq_rows >= 2 * PROMPT_CHUNK:
        L, seg = PROMPT_CHUNK, PROMPT_CHUNK
    else:
        L, seg = SAMPLE_GROUP * seq_rows, seq_rows
    h = x
    new_states = None
    cvs = []
    for layer in range(DEPTH):
        hm, hg, zg, new_states = _front(h, states, layer if stacked else 0, new_states, layer, W,
                                        L=L, seg=seg, seq_rows=seq_rows)
        if cv0 is None:
            pst = None
        else:
            pst = jnp.pad(cv0[layer], ((0, 0), (0, seq_rows - (CONV_W - 1)), (0, 0))).reshape(T, D_FF)
        h, cv = _channel(h, hm, hg, zg, p, pst, layer, W, seq_rows=seq_rows, final=layer == DEPTH - 1)
        if cv0 is None:
            cv = cv.reshape(B, -1, SUBLANE, D_FF)[:, -1, SUBLANE - (CONV_W - 1):, :]
        else:
            cv = cv.reshape(B, seq_rows, D_FF)[:, seq_rows - (CONV_W - 1):, :]
        cvs.append(cv)
    c, n, m, s = new_states
    return h, c, n, m.reshape(DEPTH, B, HEADS), s, jnp.stack(cvs)


def kernel(x_prompt, x_sample, state_mlstm_C, state_mlstm_n, state_mlstm_m, state_gla_S, state_ffn_conv, p_prompt, p_sample, norm_mix_g, w_in, b_mi, b_mf, g_mnorm, w_lr2, b_lr2, g_gnorm, w_pa, w_pb, w_o, norm_ffn_g, w_up, conv_w, conv_b, w_down, norm_ple_g, w_ple_gate, w_ple_proj, final_norm_g):
    W = _prep_weights(norm_mix_g, w_in, b_mi, b_mf, g_mnorm, w_lr2, b_lr2, g_gnorm, w_pa, w_pb, w_o,
                      norm_ffn_g, w_up, conv_w, conv_b, w_down, norm_ple_g, w_ple_gate, w_ple_proj,
                      final_norm_g)
    bp, sp, _ = x_prompt.shape
    bs, ss, _ = x_sample.shape
    zero_states = (jnp.zeros((1, bp, HEADS, DK, DV), F32), jnp.zeros((1, bp, HEADS, DK), F32),
                   jnp.zeros((1, bp, 1, HEADS), F32), jnp.zeros((1, bp, HEADS, DK, DV), F32))
    yp, c_p, n_p, m_p, s_p, cv_p = _trunk(
        x_prompt.reshape(bp * sp, D_MODEL), p_prompt.reshape(DEPTH, bp * sp, PLE_DIM),
        zero_states, False, None, W, seq_rows=sp)
    sample_states = (state_mlstm_C, state_mlstm_n, state_mlstm_m.reshape(DEPTH, bs, 1, HEADS), state_gla_S)
    ys, c_s, n_s, m_s, s_s, cv_s = _trunk(
        x_sample.reshape(bs * ss, D_MODEL), p_sample.reshape(DEPTH, bs * ss, PLE_DIM),
        sample_states, True, state_ffn_conv, W, seq_rows=ss)
    return (yp.reshape(bp, sp, D_MODEL), ys.reshape(bs, ss, D_MODEL),
            c_p, n_p, m_p, s_p, cv_p, c_s, n_s, m_s, s_s, cv_s)
```
